```python
import math
import jax, jax.numpy as jnp
from jax import lax
import numpy as np

D_MODEL = 1024
BATCH = 16
SEQ = 2048
DEPTH = 1

CHUNK = 64
PLE_DIM = 256
POOL_WIDTH = D_MODEL
POOL_WINDOWS = (2, 4, 8, 16)
N_POOL_GROUPS = 4
POOL_GROUP = POOL_WIDTH // N_POOL_GROUPS
SSM_WIDTH = D_MODEL // 2
SSM_GROUP_CH = 16
SSM_GROUPS = SSM_WIDTH // SSM_GROUP_CH
SSM_STATE = 64
DT_MIN = 1e-3
DT_MAX = 1e-1
LN_EPS = 1e-5
DEEPNORM_ALPHA = (2.0 * DEPTH) ** 0.25
DEEPNORM_BETA = (8.0 * DEPTH) ** -0.25
SPLIT_POINTS = (
    POOL_WIDTH,
    2 * POOL_WIDTH,
    2 * POOL_WIDTH + SSM_WIDTH,
    2 * POOL_WIDTH + 2 * SSM_WIDTH,
    2 * POOL_WIDTH + 2 * SSM_WIDTH + D_MODEL,
    2 * POOL_WIDTH + 2 * SSM_WIDTH + 2 * D_MODEL,
)
IN_WIDTH = 2 * POOL_WIDTH + 2 * SSM_WIDTH + 3 * D_MODEL

kernel_name = "hybrid_pool_s5_gated_deepnorm"


def layer_norm(x, g, b):
    x32 = x.astype(jnp.float32)
    mu = jnp.mean(x32, axis=-1, keepdims=True)
    var = jnp.mean(jnp.square(x32 - mu), axis=-1, keepdims=True)
    y = (x32 - mu) * lax.rsqrt(var + LN_EPS)
    return (y * g.astype(jnp.float32) + b.astype(jnp.float32)).astype(x.dtype)


def pool_mixer(u, w_groups, scale):
    bsz, s, _ = u.shape
    ug = u.astype(jnp.float32).reshape(bsz, s, N_POOL_GROUPS, POOL_GROUP)
    cs = jnp.cumsum(ug, axis=1)
    t = jnp.arange(s)
    outs = []
    for gi, w in enumerate(POOL_WINDOWS):
        c = cs[:, :, gi, :]
        prev = jnp.pad(c, ((0, 0), (w, 0), (0, 0)))[:, :s]
        cnt = jnp.minimum(t + 1, w).astype(jnp.float32)[None, :, None]
        outs.append((c - prev) / cnt - ug[:, :, gi, :])
    d = jnp.stack(outs, axis=2).astype(u.dtype)
    y = jnp.einsum("bsgi,gio->bsgo", d, w_groups).reshape(bsz, s, POOL_WIDTH)
    return y * scale


def cmul(ar, ai, br, bi):
    return ar * br - ai * bi, ar * bi + ai * br


def s5_mixer(u, a_re, a_im, log_dt, b_re, b_im, c_re, c_im, d_skip, glu_w, glu_b):
    bsz, s, _ = u.shape
    f32 = jnp.float32
    u32 = u.astype(f32).reshape(bsz, s, SSM_GROUPS, SSM_GROUP_CH)
    dt = jnp.exp(log_dt.astype(f32))[:, None]
    lr = a_re.astype(f32)
    li = a_im.astype(f32)
    mag = jnp.exp(lr * dt)
    abar_r = mag * jnp.cos(li * dt)
    abar_i = mag * jnp.sin(li * dt)
    den = lr * lr + li * li
    zr, zi = cmul(abar_r - 1.0, abar_i, lr, -li)
    zr = zr / den
    zi = zi / den
    bbar_r, bbar_i = cmul(zr[..., None], zi[..., None], b_re.astype(f32), b_im.astype(f32))
    bu_r = jnp.einsum("bsgh,gph->sbgp", u32, bbar_r)
    bu_i = jnp.einsum("bsgh,gph->sbgp", u32, bbar_i)
    a_r = jnp.broadcast_to(abar_r[None, None], (s, 1, SSM_GROUPS, SSM_STATE))
    a_i = jnp.broadcast_to(abar_i[None, None], (s, 1, SSM_GROUPS, SSM_STATE))

    def combine(left, right):
        al_r, al_i, bl_r, bl_i = left
        ar_r, ar_i, br_r, br_i = right
        na_r, na_i = cmul(ar_r, ar_i, al_r, al_i)
        t_r, t_i = cmul(ar_r, ar_i, bl_r, bl_i)
        return na_r, na_i, t_r + br_r, t_i + br_i

    _, _, xr, xi = lax.associative_scan(combine, (a_r, a_i, bu_r, bu_i), axis=0)
    y = (jnp.einsum("sbgp,ghp->bsgh", xr, c_re.astype(f32))
         - jnp.einsum("sbgp,ghp->bsgh", xi, c_im.astype(f32)))
    y = y.reshape(bsz, s, SSM_WIDTH) + d_skip.astype(f32) * u32.reshape(bsz, s, SSM_WIDTH)
    g = jax.nn.gelu(y)
    out = g * jax.nn.sigmoid(g @ glu_w.astype(f32) + glu_b.astype(f32))
    return out.astype(u.dtype)


def setup_inputs(seed: int = 0) -> dict:
    key = jax.random.key(seed)
    ks = jax.random.split(key, 24)
    f32 = jnp.float32
    L = DEPTH
    nrm = lambda k, shape: jax.random.normal(k, shape, f32)
    x = nrm(ks[0], (BATCH, SEQ, D_MODEL))
    p = nrm(ks[1], (DEPTH, BATCH, SEQ, PLE_DIM))
    w_in = nrm(ks[2], (L, D_MODEL, IN_WIDTH)) * D_MODEL ** -0.5
    pool_w = nrm(ks[3], (L, N_POOL_GROUPS, POOL_GROUP, POOL_GROUP)) * POOL_GROUP ** -0.5
    pool_scale = 1.0 + 0.02 * nrm(ks[4], (L, POOL_WIDTH))
    n = jnp.arange(SSM_STATE, dtype=f32)
    ssm_a_re = -0.5 + 0.01 * nrm(ks[5], (L, SSM_GROUPS, SSM_STATE))
    ssm_a_im = math.pi * n[None, None, :] + 0.01 * nrm(ks[6], (L, SSM_GROUPS, SSM_STATE))
    ssm_log_dt = (math.log(DT_MIN) + jax.random.uniform(ks[7], (L, SSM_GROUPS), f32)
                  * (math.log(DT_MAX) - math.log(DT_MIN)))
    b_std = (2.0 * SSM_GROUP_CH) ** -0.5
    ssm_b_re = nrm(ks[8], (L, SSM_GROUPS, SSM_STATE, SSM_GROUP_CH)) * b_std
    ssm_b_im = nrm(ks[9], (L, SSM_GROUPS, SSM_STATE, SSM_GROUP_CH)) * b_std
    c_std = (2.0 * SSM_STATE) ** -0.5
    ssm_c_re = nrm(ks[10], (L, SSM_GROUPS, SSM_GROUP_CH, SSM_STATE)) * c_std
    ssm_c_im = nrm(ks[11], (L, SSM_GROUPS, SSM_GROUP_CH, SSM_STATE)) * c_std
    ssm_d = nrm(ks[12], (L, SSM_WIDTH))
    glu_w = nrm(ks[13], (L, SSM_WIDTH, SSM_WIDTH)) * SSM_WIDTH ** -0.5
    glu_b = 0.01 * nrm(ks[14], (L, SSM_WIDTH))
    w_branch_pool = nrm(ks[15], (L, POOL_WIDTH, D_MODEL)) * POOL_WIDTH ** -0.5 * DEEPNORM_BETA
    w_branch_ssm = nrm(ks[16], (L, SSM_WIDTH, D_MODEL)) * SSM_WIDTH ** -0.5 * DEEPNORM_BETA
    w_out = nrm(ks[17], (L, D_MODEL, D_MODEL)) * D_MODEL ** -0.5 * DEEPNORM_BETA
    w_ple = nrm(ks[18], (L, PLE_DIM, D_MODEL)) * PLE_DIM ** -0.5 * DEEPNORM_BETA
    ln_g = 1.0 + 0.02 * nrm(ks[19], (L, D_MODEL))
    ln_b = 0.01 * nrm(ks[20], (L, D_MODEL))
    return {"x": x, "p": p, "w_in": w_in, "pool_w": pool_w, "pool_scale": pool_scale,
            "ssm_a_re": ssm_a_re, "ssm_a_im": ssm_a_im, "ssm_log_dt": ssm_log_dt,
            "ssm_b_re": ssm_b_re, "ssm_b_im": ssm_b_im, "ssm_c_re": ssm_c_re, "ssm_c_im": ssm_c_im,
            "ssm_d": ssm_d, "glu_w": glu_w, "glu_b": glu_b,
            "w_branch_pool": w_branch_pool, "w_branch_ssm": w_branch_ssm, "w_out": w_out,
            "w_ple": w_ple, "ln_g": ln_g, "ln_b": ln_b}


def reference(x, p, w_in, pool_w, pool_scale, ssm_a_re, ssm_a_im, ssm_log_dt,
              ssm_b_re, ssm_b_im, ssm_c_re, ssm_c_im, ssm_d, glu_w, glu_b,
              w_branch_pool, w_branch_ssm, w_out, w_ple, ln_g, ln_b):
    for i in range(DEPTH):
        proj = jnp.einsum("bsd,dn->bsn", x, w_in[i])
        pool_in, pool_gate, ssm_in, ssm_gate, g_pool, g_ssm, ple_gate = jnp.split(
            proj, SPLIT_POINTS, axis=-1)
        y_pool = pool_mixer(pool_in, pool_w[i], pool_scale[i]) * jax.nn.silu(pool_gate)
        y_ssm = s5_mixer(ssm_in, ssm_a_re[i], ssm_a_im[i], ssm_log_dt[i], ssm_b_re[i],
                         ssm_b_im[i], ssm_c_re[i], ssm_c_im[i], ssm_d[i], glu_w[i],
                         glu_b[i]) * jax.nn.silu(ssm_gate)
        merged = (jax.nn.sigmoid(g_pool) * (y_pool @ w_branch_pool[i])
                  + jax.nn.sigmoid(g_ssm) * (y_ssm @ w_branch_ssm[i]))
        mix = merged @ w_out[i]
        ple = jax.nn.sigmoid(ple_gate) * (p[i] @ w_ple[i])
        x = layer_norm(DEEPNORM_ALPHA * x + mix + ple, ln_g[i], ln_b[i])
    return x
```

```python
import functools
import math

import jax
import jax.numpy as jnp
from jax import lax
from jax.experimental import pallas as pl
from jax.experimental.pallas import tpu as pltpu

D_MODEL = 1024
PLE_DIM = 256
POOL_WIDTH = D_MODEL
POOL_WINDOWS = (2, 4, 8, 16)
POOL_GROUP = POOL_WIDTH // len(POOL_WINDOWS)
POOL_HALO = max(POOL_WINDOWS)
SSM_WIDTH = D_MODEL // 2
SSM_GROUP_CH = 16
SSM_GROUPS = SSM_WIDTH // SSM_GROUP_CH
SSM_STATE = 64
LN_EPS = 1e-5

_OFF_POOL_IN = 0
_OFF_POOL_GATE = POOL_WIDTH
_OFF_SSM_IN = 2 * POOL_WIDTH
_OFF_SSM_GATE = 2 * POOL_WIDTH + SSM_WIDTH
_OFF_G_POOL = 2 * POOL_WIDTH + 2 * SSM_WIDTH
_OFF_G_SSM = _OFF_G_POOL + D_MODEL
_OFF_PLE_GATE = _OFF_G_SSM + D_MODEL
IN_WIDTH = _OFF_PLE_GATE + D_MODEL

LANES = 128
SUBLANES = 8
MXU_DIM = 256
VMEM_LIMIT_BYTES = 56 * 1024 * 1024

GROUPS_PER_PAIR = LANES // SSM_STATE
N_PAIRS = SSM_GROUPS // GROUPS_PER_PAIR
PAIR_COLS = 2 * LANES
STATE_COLS = N_PAIRS * PAIR_COLS
SCAN_PAIRS = 4
GROUPS_PER_LANE_TILE = LANES // SSM_GROUP_CH
SSM_LANE_TILES = SSM_WIDTH // LANES
WC_TILES = SSM_WIDTH // MXU_DIM

TIME_TILE = 16
ROW_PAD = 8

_BF16 = jnp.bfloat16
_F32 = jnp.float32


def _sigmoid(v):
    return 0.5 * (jnp.tanh(0.5 * v) + 1.0)


def _silu(v):
    return v * _sigmoid(v)


def _gelu_tanh(v):
    c = math.sqrt(2.0 / math.pi)
    return 0.5 * v * (1.0 + jnp.tanh(c * (v + 0.044715 * (v * v * v))))


def _dot(a, b):
    return jnp.dot(a, b, preferred_element_type=_F32)


def _layer_kernel(x_ref, p_ref, w_in_ref, pool_w_ref, pool_scale_ref, wb_ref, wc_ref,
                  ar_ref, ai_ref, dskip_ref, glu_w_ref, glu_b_ref, wbp_ref, wbs_ref,
                  w_out_ref, w_ple_ref, lng_ref, lnb_ref, o_ref,
                  halo_ref, ut_ref, utb_ref, bu_ref, zb_ref, zstate_ref, yt_ref,
                  *, batch, tt, alpha):
    rows = batch * tt
    pitch = tt + ROW_PAD
    step = pl.program_id(0)

    @pl.when(step == 0)
    def _():
        halo_ref[:, 0:POOL_HALO, :] = jnp.zeros((batch, POOL_HALO, POOL_WIDTH), _F32)
        zstate_ref[...] = jnp.zeros_like(zstate_ref)

    xb = x_ref[...].reshape(rows, D_MODEL).astype(_BF16)

    def proj(off, width):
        return _dot(xb, w_in_ref[:, off:off + width])

    pool_in = proj(_OFF_POOL_IN, POOL_WIDTH)
    halo_ref[:, POOL_HALO:POOL_HALO + tt, :] = pool_in.reshape(batch, tt, POOL_WIDTH)
    t_glob = step * tt + lax.broadcasted_iota(jnp.int32, (1, tt, POOL_GROUP), 1)
    pooled = []
    for gi, win in enumerate(POOL_WINDOWS):
        c0 = gi * POOL_GROUP
        acc = None
        for k in range(win):
            sh = halo_ref[:, POOL_HALO - k:POOL_HALO - k + tt, c0:c0 + POOL_GROUP]
            acc = sh if acc is None else acc + sh
        inv_cnt = 1.0 / jnp.minimum(t_glob + 1, win).astype(_F32)
        cur = halo_ref[:, POOL_HALO:POOL_HALO + tt, c0:c0 + POOL_GROUP]
        d = (acc * inv_cnt - cur).reshape(rows, POOL_GROUP)
        yg = _dot(d.astype(_BF16), pool_w_ref[gi])
        pooled.append(yg * pool_scale_ref[:, c0:c0 + POOL_GROUP])
    halo_ref[:, 0:POOL_HALO, :] = halo_ref[:, tt:tt + POOL_HALO, :]
    y_pool = jnp.concatenate(pooled, axis=1) * _silu(proj(_OFF_POOL_GATE, POOL_WIDTH))
    merged = _sigmoid(proj(_OFF_G_POOL, D_MODEL)) * _dot(y_pool.astype(_BF16), wbp_ref[...])

    u = proj(_OFF_SSM_IN, SSM_WIDTH)
    for c in range(SSM_LANE_TILES):
        for b in range(batch):
            ut_ref[c, b * pitch:b * pitch + tt, :] = (
                u[b * tt:(b + 1) * tt, c * LANES:(c + 1) * LANES])

    def to_time_major(t, carry):
        r0 = pl.multiple_of(t * batch, batch)
        for c in range(SSM_LANE_TILES):
            v = ut_ref[c, pl.ds(t, batch, stride=pitch), :]
            utb_ref[pl.ds(r0, batch), c * LANES:(c + 1) * LANES] = v.astype(_BF16)
        return carry

    lax.fori_loop(0, tt, to_time_major, 0)

    pairs_per_tile = GROUPS_PER_LANE_TILE // GROUPS_PER_PAIR
    for j in range(N_PAIRS):
        c = j // pairs_per_tile
        bu_ref[:, j * PAIR_COLS:(j + 1) * PAIR_COLS] = _dot(
            utb_ref[:, c * LANES:(c + 1) * LANES], wb_ref[j])

    blk_cols = SCAN_PAIRS * PAIR_COLS
    for blk in range(N_PAIRS // SCAN_PAIRS):
        l0 = blk * blk_cols
        coef = []
        for q in range(SCAN_PAIRS):
            j = blk * SCAN_PAIRS + q
            coef.append((ar_ref[j:j + 1, :], ai_ref[j:j + 1, :]))

        def scan_step(t, z, l0=l0, coef=coef):
            r0 = pl.multiple_of(t * batch, batch)
            bu = bu_ref[pl.ds(r0, batch), l0:l0 + blk_cols]
            out = []
            for q in range(SCAN_PAIRS):
                a_r, a_i = coef[q]
                o = q * PAIR_COLS
                z_r = z[:, o:o + LANES]
                z_i = z[:, o + LANES:o + PAIR_COLS]
                out.append(a_r * z_r - a_i * z_i + bu[:, o:o + LANES])
                out.append(a_r * z_i + a_i * z_r + bu[:, o + LANES:o + PAIR_COLS])
            z_new = jnp.concatenate(out, axis=1)
            zb_ref[pl.ds(r0, batch), l0:l0 + blk_cols] = z_new.astype(_BF16)
            return z_new

        z_fin = lax.fori_loop(0, tt, scan_step, zstate_ref[:, l0:l0 + blk_cols])
        zstate_ref[:, l0:l0 + blk_cols] = z_fin

    k_per_tile = STATE_COLS // WC_TILES
    for n in range(WC_TILES):
        y_tb = _dot(zb_ref[:, n * k_per_tile:(n + 1) * k_per_tile], wc_ref[n])
        for t in range(tt):
            for h in range(MXU_DIM // LANES):
                c = n * (MXU_DIM // LANES) + h
                yt_ref[c, pl.ds(t, batch, stride=pitch), :] = (
                    y_tb[t * batch:(t + 1) * batch, h * LANES:(h + 1) * LANES])
    y_cols = []
    for c in range(SSM_LANE_TILES):
        y_cols.append(jnp.concatenate(
            [yt_ref[c, b * pitch:b * pitch + tt, :] for b in range(batch)], axis=0))
    y = jnp.concatenate(y_cols, axis=1) + dskip_ref[...] * u
    g = _gelu_tanh(y)
    glu = g * _sigmoid(_dot(g.astype(_BF16), glu_w_ref[...]) + glu_b_ref[...])
    y_ssm = glu * _silu(proj(_OFF_SSM_GATE, SSM_WIDTH))
    merged = merged + _sigmoid(proj(_OFF_G_SSM, D_MODEL)) * _dot(
        y_ssm.astype(_BF16), wbs_ref[...])

    mix = _dot(merged.astype(_BF16), w_out_ref[...])
    pb = p_ref[...].reshape(rows, PLE_DIM).astype(_BF16)
    ple = _sigmoid(proj(_OFF_PLE_GATE, D_MODEL)) * _dot(pb, w_ple_ref[...])
    h = alpha * x_ref[...].reshape(rows, D_MODEL) + mix + ple
    mu = jnp.mean(h, axis=-1, keepdims=True)
    hc = h - mu
    var = jnp.mean(hc * hc, axis=-1, keepdims=True)
    out = hc * lax.rsqrt(var + LN_EPS) * lng_ref[...] + lnb_ref[...]
    o_ref[...] = out.reshape(batch, tt, D_MODEL).astype(o_ref.dtype)


def _cmul(ar, ai, br, bi):
    return ar * br - ai * bi, ar * bi + ai * br


def _ssm_operands(a_re, a_im, log_dt, b_re, b_im, c_re, c_im):
    f32 = _F32
    dt = jnp.exp(log_dt.astype(f32))[:, None]
    lr = a_re.astype(f32)
    li = a_im.astype(f32)
    mag = jnp.exp(lr * dt)
    abar_r = mag * jnp.cos(li * dt)
    abar_i = mag * jnp.sin(li * dt)
    den = lr * lr + li * li
    zr, zi = _cmul(abar_r - 1.0, abar_i, lr, -li)
    zr = zr / den
    zi = zi / den
    bbar_r, bbar_i = _cmul(zr[..., None], zi[..., None], b_re.astype(f32), b_im.astype(f32))

    ar = abar_r.reshape(N_PAIRS, LANES)
    ai = abar_i.reshape(N_PAIRS, LANES)

    pairs_per_tile = GROUPS_PER_LANE_TILE // GROUPS_PER_PAIR
    bb = jnp.stack([bbar_r, bbar_i]).reshape(
        2, N_PAIRS, GROUPS_PER_PAIR, SSM_STATE, SSM_GROUP_CH)
    jj = jnp.arange(N_PAIRS)[:, None, None]
    qq = jnp.arange(GROUPS_PER_PAIR)[None, :, None]
    gg = jnp.arange(GROUPS_PER_LANE_TILE)[None, None, :]
    sel_b = (gg == (jj % pairs_per_tile) * GROUPS_PER_PAIR + qq).astype(f32)
    wb = jnp.einsum("jqg,rjqph->jghrqp", sel_b, bb).reshape(N_PAIRS, LANES, PAIR_COLS)

    pairs_per_wc = N_PAIRS // WC_TILES
    groups_per_wc = SSM_GROUPS // WC_TILES
    cc = jnp.stack([c_re.astype(f32), -c_im.astype(f32)]).reshape(
        2, WC_TILES, pairs_per_wc, GROUPS_PER_PAIR, SSM_GROUP_CH, SSM_STATE)
    ll = jnp.arange(pairs_per_wc)[:, None, None]
    g16 = jnp.arange(groups_per_wc)[None, None, :]
    sel_c = (g16 == ll * GROUPS_PER_PAIR + qq).astype(f32)
    wc = jnp.einsum("lqg,rnlqhp->nlrqpgh", sel_c, cc).reshape(
        WC_TILES, pairs_per_wc * PAIR_COLS, MXU_DIM)
    return ar, ai, wb.astype(_BF16), wc.astype(_BF16)


def _const_spec(shape):
    zeros = (0,) * len(shape)
    return pl.BlockSpec(shape, lambda i, _z=zeros: _z, pipeline_mode=pl.Buffered(1))


def _layer(x, p, w_in, pool_w, pool_scale, ssm, d_skip, glu_w, glu_b,
           w_branch_pool, w_branch_ssm, w_out, w_ple, ln_g, ln_b, alpha):
    batch, seq, d_model = x.shape
    assert d_model == D_MODEL and w_in.shape == (D_MODEL, IN_WIDTH)
    tt = TIME_TILE
    assert seq % tt == 0 and tt % SUBLANES == 0 and tt >= POOL_HALO
    assert batch % SUBLANES == 0
    rows = batch * tt
    pitch = tt + ROW_PAD
    ar, ai, wb, wc = ssm
    row = lambda v: v.astype(_F32).reshape(1, -1)
    operands = (
        x, p,
        w_in.astype(_BF16), pool_w.astype(_BF16), row(pool_scale), wb, wc, ar, ai,
        row(d_skip), glu_w.astype(_BF16), row(glu_b),
        w_branch_pool.astype(_BF16), w_branch_ssm.astype(_BF16),
        w_out.astype(_BF16), w_ple.astype(_BF16), row(ln_g), row(ln_b),
    )
    in_specs = [
        pl.BlockSpec((batch, tt, D_MODEL), lambda i: (0, i, 0)),
        pl.BlockSpec((batch, tt, PLE_DIM), lambda i: (0, i, 0)),
    ] + [_const_spec(op.shape) for op in operands[2:]]
    kernel = functools.partial(_layer_kernel, batch=batch, tt=tt, alpha=alpha)
    return pl.pallas_call(
        kernel,
        grid=(seq // tt,),
        in_specs=in_specs,
        out_specs=pl.BlockSpec((batch, tt, D_MODEL), lambda i: (0, i, 0)),
        out_shape=jax.ShapeDtypeStruct(x.shape, x.dtype),
        scratch_shapes=[
            pltpu.VMEM((batch, POOL_HALO + tt, POOL_WIDTH), _F32),
            pltpu.VMEM((SSM_LANE_TILES, batch * pitch, LANES), _F32),
            pltpu.VMEM((rows, SSM_WIDTH), _BF16),
            pltpu.VMEM((rows, STATE_COLS), _F32),
            pltpu.VMEM((rows, STATE_COLS), _BF16),
            pltpu.VMEM((batch, STATE_COLS), _F32),
            pltpu.VMEM((SSM_LANE_TILES, batch * pitch, LANES), _F32),
        ],
        compiler_params=pltpu.CompilerParams(
            dimension_semantics=("arbitrary",),
            vmem_limit_bytes=VMEM_LIMIT_BYTES,
        ),
        name="hybrid_pool_s5_layer",
    )(*operands)


def kernel(x, p, w_in, pool_w, pool_scale, ssm_a_re, ssm_a_im, ssm_log_dt, ssm_b_re, ssm_b_im, ssm_c_re, ssm_c_im, ssm_d, glu_w, glu_b, w_branch_pool, w_branch_ssm, w_out, w_ple, ln_g, ln_b):
    depth = w_in.shape[0]
    alpha = (2.0 * depth) ** 0.25
    for i in range(depth):
        ssm = _ssm_operands(ssm_a_re[i], ssm_a_im[i], ssm_log_dt[i], ssm_b_re[i],
                            ssm_b_im[i], ssm_c_re[i], ssm_c_im[i])
        x = _layer(x, p[i], w_in[i], pool_w[i], pool_scale[i], ssm, ssm_d[i], glu_w[i],
                   glu_b[i], w_branch_pool[i], w_branch_ssm[i], w_out[i], w_ple[i],
                   ln_g[i], ln_b[i], alpha)
    return x
```

```python
import functools
import math

import jax
import jax.numpy as jnp
from jax import lax
from jax.experimental import pallas as pl
from jax.experimental.pallas import tpu as pltpu

D_MODEL = 1024
PLE_DIM = 256
POOL_WIDTH = D_MODEL
POOL_WINDOWS = (2, 4, 8, 16)
POOL_GROUP = POOL_WIDTH // len(POOL_WINDOWS)
POOL_HALO = max(POOL_WINDOWS)
SSM_WIDTH = D_MODEL // 2
SSM_GROUP_CH = 16
SSM_GROUPS = SSM_WIDTH // SSM_GROUP_CH
SSM_STATE = 64
LN_EPS = 1e-5

_OFF_POOL_IN = 0
_OFF_POOL_GATE = POOL_WIDTH
_OFF_SSM_IN = 2 * POOL_WIDTH
_OFF_SSM_GATE = 2 * POOL_WIDTH + SSM_WIDTH
_OFF_G_POOL = 2 * POOL_WIDTH + 2 * SSM_WIDTH
_OFF_G_SSM = _OFF_G_POOL + D_MODEL
_OFF_PLE_GATE = _OFF_G_SSM + D_MODEL
IN_WIDTH = _OFF_PLE_GATE + D_MODEL

LANES = 128
SUBLANES = 8
MXU_DIM = 256
VMEM_LIMIT_BYTES = 56 * 1024 * 1024

GROUPS_PER_PAIR = LANES // SSM_STATE
N_PAIRS = SSM_GROUPS // GROUPS_PER_PAIR
PAIR_COLS = 2 * LANES
STATE_COLS = N_PAIRS * PAIR_COLS
SCAN_PAIRS = 4
SCAN_BLOCKS = N_PAIRS // SCAN_PAIRS
BLOCK_COLS = SCAN_PAIRS * PAIR_COLS
GROUPS_PER_LANE_TILE = LANES // SSM_GROUP_CH
PAIRS_PER_LANE_TILE = GROUPS_PER_LANE_TILE // GROUPS_PER_PAIR
SSM_LANE_TILES = SSM_WIDTH // LANES
WC_TILES = SSM_WIDTH // MXU_DIM
BLOCKS_PER_WC = SCAN_BLOCKS // WC_TILES

TIME_TILE = 32
ROW_PAD = 8
ROW_SPLITS = 2

_BF16 = jnp.bfloat16
_F32 = jnp.float32
_GELU_C = math.sqrt(2.0 / math.pi)


def _dot(a, b):
    return jnp.dot(a, b, preferred_element_type=_F32)


def _half_silu(h):
    return h + h * jnp.tanh(h)


def _layer_kernel(x_ref, p_ref, w_in_ref, pool_w_ref, pool_scale_ref, wb_ref, wc_ref,
                  ar_ref, ai_ref, dskip_ref, glu_w_ref, glu_b_ref, wbp_ref, wbs_ref,
                  w_out_ref, w_ple_ref, lng_ref, lnb_ref, o_ref,
                  halo_ref, ut_ref, utb_ref, zstate_ref, yt_ref,
                  *, batch, tt, alpha):
    rows = batch * tt
    pitch = tt + ROW_PAD
    step = pl.program_id(0)

    @pl.when(step == 0)
    def _():
        halo_ref[:, 0:POOL_HALO, :] = jnp.zeros((batch, POOL_HALO, POOL_WIDTH), _F32)
        zstate_ref[...] = jnp.zeros_like(zstate_ref)

    xb = x_ref[...].reshape(rows, D_MODEL).astype(_BF16)

    def proj(off, width):
        return _dot(xb, w_in_ref[:, off:off + width])

    def expand(blk):
        cols = []
        for q in range(SCAN_PAIRS):
            j = blk * SCAN_PAIRS + q
            c = j // PAIRS_PER_LANE_TILE
            cols.append(_dot(utb_ref[:, c * LANES:(c + 1) * LANES], wb_ref[j]))
        return jnp.concatenate(cols, axis=1)

    def scan(blk, bu):
        l0 = blk * BLOCK_COLS
        z = zstate_ref[:, l0:l0 + BLOCK_COLS]
        steps = []
        for t in range(tt):
            but = bu[t * batch:(t + 1) * batch, :]
            out = []
            for q in range(SCAN_PAIRS):
                j = blk * SCAN_PAIRS + q
                a_r = ar_ref[j:j + 1, :]
                a_i = ai_ref[j:j + 1, :]
                o = q * PAIR_COLS
                z_r = z[:, o:o + LANES]
                z_i = z[:, o + LANES:o + PAIR_COLS]
                out.append(a_r * z_r - a_i * z_i + but[:, o:o + LANES])
                out.append(a_r * z_i + a_i * z_r + but[:, o + LANES:o + PAIR_COLS])
            z = jnp.concatenate(out, axis=1)
            steps.append(z.astype(_BF16))
        zstate_ref[:, l0:l0 + BLOCK_COLS] = z
        return jnp.concatenate(steps, axis=0)

    def contract(blk, zb):
        n = blk // BLOCKS_PER_WC
        k0 = (blk % BLOCKS_PER_WC) * BLOCK_COLS
        return _dot(zb, wc_ref[n, k0:k0 + BLOCK_COLS, :])

    u = proj(_OFF_SSM_IN, SSM_WIDTH)
    pool_in = proj(_OFF_POOL_IN, POOL_WIDTH)

    for c in range(SSM_LANE_TILES):
        for b in range(batch):
            ut_ref[c, b * pitch:b * pitch + tt, :] = (
                u[b * tt:(b + 1) * tt, c * LANES:(c + 1) * LANES])
    for t in range(tt):
        for c in range(SSM_LANE_TILES):
            v = ut_ref[c, pl.ds(t, batch, stride=pitch), :]
            utb_ref[t * batch:(t + 1) * batch, c * LANES:(c + 1) * LANES] = v.astype(_BF16)

    halo_ref[:, POOL_HALO:POOL_HALO + tt, :] = pool_in.reshape(batch, tt, POOL_WIDTH)
    bu0 = expand(0)
    bu1 = expand(1)

    t_glob = step * tt + lax.broadcasted_iota(jnp.int32, (1, tt, POOL_GROUP), 1)
    d_blocks = []
    for gi, win in enumerate(POOL_WINDOWS):
        c0 = gi * POOL_GROUP
        s = halo_ref[:, :, c0:c0 + POOL_GROUP]
        cur = s[:, POOL_HALO:, :]
        k = 1
        while k < win:
            s = s + pltpu.roll(s, k, axis=1)
            k *= 2
        inv_cnt = 1.0 / jnp.minimum(t_glob + 1, win).astype(_F32)
        d = s[:, POOL_HALO:, :] * inv_cnt - cur
        d_blocks.append(d.reshape(rows, POOL_GROUP).astype(_BF16))
    halo_ref[:, 0:POOL_HALO, :] = halo_ref[:, tt:tt + POOL_HALO, :]

    h_pool = proj(_OFF_POOL_GATE, POOL_WIDTH)
    zb0 = scan(0, bu0)
    bu2 = expand(2)
    bu3 = expand(3)
    pooled = [_dot(d_blocks[gi], pool_w_ref[gi])
              * pool_scale_ref[:, gi * POOL_GROUP:(gi + 1) * POOL_GROUP]
              for gi in range(len(POOL_WINDOWS))]
    t_gpool = jnp.tanh(proj(_OFF_G_POOL, D_MODEL))
    zb1 = scan(1, bu1)
    y_tb0 = contract(0, zb0)
    y_pool = (jnp.concatenate(pooled, axis=1) * _half_silu(h_pool)).astype(_BF16)
    half_bp = _dot(y_pool, wbp_ref[...])
    zb2 = scan(2, bu2)
    h_ssm = proj(_OFF_SSM_GATE, SSM_WIDTH)
    y_tb0 = y_tb0 + contract(1, zb1)
    merged_pool = half_bp + t_gpool * half_bp
    zb3 = scan(3, bu3)
    t_gssm = jnp.tanh(proj(_OFF_G_SSM, D_MODEL))
    y_tb1 = contract(2, zb2) + contract(3, zb3)

    for n, y_tb in enumerate((y_tb0, y_tb1)):
        for t in range(tt):
            for hh in range(MXU_DIM // LANES):
                c = n * (MXU_DIM // LANES) + hh
                yt_ref[c, pl.ds(t, batch, stride=pitch), :] = (
                    y_tb[t * batch:(t + 1) * batch, hh * LANES:(hh + 1) * LANES])
    y_cols = []
    for c in range(SSM_LANE_TILES):
        y_cols.append(jnp.concatenate(
            [yt_ref[c, b * pitch:b * pitch + tt, :] for b in range(batch)], axis=0))
    y = jnp.concatenate(y_cols, axis=1) + dskip_ref[...] * u
    half_y = 0.5 * y
    g = half_y + half_y * jnp.tanh(y * (_GELU_C + (_GELU_C * 0.044715) * (y * y)))
    t_ple = jnp.tanh(proj(_OFF_PLE_GATE, D_MODEL))
    half_q = _dot(g.astype(_BF16), glu_w_ref[...]) + glu_b_ref[...]
    pb = p_ref[...].reshape(rows, PLE_DIM).astype(_BF16)
    half_ple = _dot(pb, w_ple_ref[...])
    half_g = 0.5 * g
    y_ssm = ((half_g + half_g * jnp.tanh(half_q)) * _half_silu(h_ssm)).astype(_BF16)
    ple = half_ple + t_ple * half_ple

    bsplit = batch // ROW_SPLITS
    rsplit = rows // ROW_SPLITS
    for r in range(ROW_SPLITS):
        rs = slice(r * rsplit, (r + 1) * rsplit)
        bs = slice(r * bsplit, (r + 1) * bsplit)
        half_bs = _dot(y_ssm[rs], wbs_ref[...])
        merged = merged_pool[rs] + half_bs + t_gssm[rs] * half_bs
        mix = _dot(merged.astype(_BF16), w_out_ref[...])
        h = alpha * x_ref[bs].reshape(rsplit, D_MODEL) + mix + ple[rs]
        mu = jnp.mean(h, axis=-1, keepdims=True)
        hc = h - mu
        var = jnp.mean(hc * hc, axis=-1, keepdims=True)
        out = hc * lax.rsqrt(var + LN_EPS) * lng_ref[...] + lnb_ref[...]
        o_ref[bs] = out.reshape(bsplit, tt, D_MODEL).astype(o_ref.dtype)


def _cmul(ar, ai, br, bi):
    return ar * br - ai * bi, ar * bi + ai * br


def _ssm_operands(a_re, a_im, log_dt, b_re, b_im, c_re, c_im):
    f32 = _F32
    dt = jnp.exp(log_dt.astype(f32))[:, None]
    lr = a_re.astype(f32)
    li = a_im.astype(f32)
    mag = jnp.exp(lr * dt)
    abar_r = mag * jnp.cos(li * dt)
    abar_i = mag * jnp.sin(li * dt)
    den = lr * lr + li * li
    zr, zi = _cmul(abar_r - 1.0, abar_i, lr, -li)
    zr = zr / den
    zi = zi / den
    bbar_r, bbar_i = _cmul(zr[..., None], zi[..., None], b_re.astype(f32), b_im.astype(f32))

    ar = abar_r.reshape(N_PAIRS, LANES)
    ai = abar_i.reshape(N_PAIRS, LANES)

    bb = jnp.stack([bbar_r, bbar_i]).reshape(
        2, N_PAIRS, GROUPS_PER_PAIR, SSM_STATE, SSM_GROUP_CH)
    jj = jnp.arange(N_PAIRS)[:, None, None]
    qq = jnp.arange(GROUPS_PER_PAIR)[None, :, None]
    gg = jnp.arange(GROUPS_PER_LANE_TILE)[None, None, :]
    sel_b = (gg == (jj % PAIRS_PER_LANE_TILE) * GROUPS_PER_PAIR + qq).astype(f32)
    wb = jnp.einsum("jqg,rjqph->jghrqp", sel_b, bb).reshape(N_PAIRS, LANES, PAIR_COLS)

    pairs_per_wc = N_PAIRS // WC_TILES
    groups_per_wc = SSM_GROUPS // WC_TILES
    cc = jnp.stack([c_re.astype(f32), -c_im.astype(f32)]).reshape(
        2, WC_TILES, pairs_per_wc, GROUPS_PER_PAIR, SSM_GROUP_CH, SSM_STATE)
    ll = jnp.arange(pairs_per_wc)[:, None, None]
    g16 = jnp.arange(groups_per_wc)[None, None, :]
    sel_c = (g16 == ll * GROUPS_PER_PAIR + qq).astype(f32)
    wc = jnp.einsum("lqg,rnlqhp->nlrqpgh", sel_c, cc).reshape(
        WC_TILES, pairs_per_wc * PAIR_COLS, MXU_DIM)
    return ar, ai, wb.astype(_BF16), wc.astype(_BF16)


def _const_spec(shape):
    zeros = (0,) * len(shape)
    return pl.BlockSpec(shape, lambda i, _z=zeros: _z, pipeline_mode=pl.Buffered(1))


def _layer(x, p, w_in, pool_w, pool_scale, ssm, d_skip, glu_w, glu_b,
           w_branch_pool, w_branch_ssm, w_out, w_ple, ln_g, ln_b, alpha):
    batch, seq, d_model = x.shape
    assert d_model == D_MODEL and w_in.shape == (D_MODEL, IN_WIDTH)
    tt = TIME_TILE
    assert seq % tt == 0 and tt % SUBLANES == 0 and tt >= POOL_HALO
    assert batch % (SUBLANES * ROW_SPLITS) == 0
    rows = batch * tt
    pitch = tt + ROW_PAD
    ar, ai, wb, wc = ssm
    row = lambda v: v.astype(_F32).reshape(1, -1)
    col = jnp.arange(IN_WIDTH)
    is_value = (col < _OFF_POOL_GATE) | ((col >= _OFF_SSM_IN) & (col < _OFF_SSM_GATE))
    gate_scale = jnp.where(is_value, 1.0, 0.5).astype(_F32)
    operands = (
        x, p,
        (w_in * gate_scale).astype(_BF16), pool_w.astype(_BF16), row(pool_scale),
        wb, wc, ar, ai, row(d_skip),
        (0.5 * glu_w).astype(_BF16), row(0.5 * glu_b),
        (0.5 * w_branch_pool).astype(_BF16), (0.5 * w_branch_ssm).astype(_BF16),
        w_out.astype(_BF16), (0.5 * w_ple).astype(_BF16), row(ln_g), row(ln_b),
    )
    in_specs = [
        pl.BlockSpec((batch, tt, D_MODEL), lambda i: (0, i, 0)),
        pl.BlockSpec((batch, tt, PLE_DIM), lambda i: (0, i, 0)),
    ] + [_const_spec(op.shape) for op in operands[2:]]
    kernel = functools.partial(_layer_kernel, batch=batch, tt=tt, alpha=alpha)
    return pl.pallas_call(
        kernel,
        grid=(seq // tt,),
        in_specs=in_specs,
        out_specs=pl.BlockSpec((batch, tt, D_MODEL), lambda i: (0, i, 0)),
        out_shape=jax.ShapeDtypeStruct(x.shape, x.dtype),
        scratch_shapes=[
            pltpu.VMEM((batch, POOL_HALO + tt, POOL_WIDTH), _F32),
            pltpu.VMEM((SSM_LANE_TILES, batch * pitch, LANES), _F32),
            pltpu.VMEM((rows, SSM_WIDTH), _BF16),
            pltpu.VMEM((batch, STATE_COLS), _F32),
            pltpu.VMEM((SSM_LANE_TILES, batch * pitch, LANES), _F32),
        ],
        compiler_params=pltpu.CompilerParams(
            dimension_semantics=("arbitrary",),
            vmem_limit_bytes=VMEM_LIMIT_BYTES,
        ),
        name="hybrid_pool_s5_layer",
    )(*operands)


def kernel(x, p, w_in, pool_w, pool_scale, ssm_a_re, ssm_a_im, ssm_log_dt, ssm_b_re, ssm_b_im, ssm_c_re, ssm_c_im, ssm_d, glu_w, glu_b, w_branch_pool, w_branch_ssm, w_out, w_ple, ln_g, ln_b):
    depth = w_in.shape[0]
    alpha = (2.0 * depth) ** 0.25
    for i in range(depth):
        ssm = _ssm_operands(ssm_a_re[i], ssm_a_im[i], ssm_log_dt[i], ssm_b_re[i],
                            ssm_b_im[i], ssm_c_re[i], ssm_c_im[i])
        x = _layer(x, p[i], w_in[i], pool_w[i], pool_scale[i], ssm, ssm_d[i], glu_w[i],
                   glu_b[i], w_branch_pool[i], w_branch_ssm[i], w_out[i], w_ple[i],
                   ln_g[i], ln_b[i], alpha)
    return x
```

```python
import functools
import math

import jax
import jax.numpy as jnp
from jax import lax
from jax.experimental import pallas as pl
from jax.experimental.pallas import tpu as pltpu

D_MODEL = 1024
PLE_DIM = 256
POOL_WIDTH = D_MODEL
POOL_WINDOWS = (2, 4, 8, 16)
POOL_GROUP = POOL_WIDTH // len(POOL_WINDOWS)
POOL_HALO = max(POOL_WINDOWS)
SSM_WIDTH = D_MODEL // 2
SSM_GROUP_CH = 16
SSM_GROUPS = SSM_WIDTH // SSM_GROUP_CH
SSM_STATE = 64
LN_EPS = 1e-5

_OFF_POOL_IN = 0
_OFF_POOL_GATE = POOL_WIDTH
_OFF_SSM_IN = 2 * POOL_WIDTH
_OFF_SSM_GATE = 2 * POOL_WIDTH + SSM_WIDTH
_OFF_G_POOL = 2 * POOL_WIDTH + 2 * SSM_WIDTH
_OFF_G_SSM = _OFF_G_POOL + D_MODEL
_OFF_PLE_GATE = _OFF_G_SSM + D_MODEL
IN_WIDTH = _OFF_PLE_GATE + D_MODEL

LANES = 128
SUBLANES = 8
BF16_SUBLANES = 16
MXU_DIM = 256
VMEM_LIMIT_BYTES = 56 * 1024 * 1024

GROUPS_PER_PAIR = LANES // SSM_STATE
N_PAIRS = SSM_GROUPS // GROUPS_PER_PAIR
PAIR_COLS = 2 * LANES
STATE_COLS = N_PAIRS * PAIR_COLS
SCAN_PAIRS = 4
SCAN_BLOCKS = N_PAIRS // SCAN_PAIRS
BLOCK_COLS = SCAN_PAIRS * PAIR_COLS
GROUPS_PER_LANE_TILE = LANES // SSM_GROUP_CH
PAIRS_PER_LANE_TILE = GROUPS_PER_LANE_TILE // GROUPS_PER_PAIR
SSM_LANE_TILES = SSM_WIDTH // LANES
WC_TILES = SSM_WIDTH // MXU_DIM
BLOCKS_PER_WC = SCAN_BLOCKS // WC_TILES

TIME_TILE = 32
ROW_PAD = 8
ROW_SPLITS = 2
WEIGHT_STEPS = 16

_BF16 = jnp.bfloat16
_F32 = jnp.float32
_GELU_C = math.sqrt(2.0 / math.pi)
_CONTRACT_LAST = (((1,), (1,)), ((), ()))


def _dot(a, b):
    return jnp.dot(a, b, preferred_element_type=_F32)


def _dot_t(a, b_t):
    return lax.dot_general(a, b_t, _CONTRACT_LAST, preferred_element_type=_F32)


def _half_silu(h):
    return h + h * jnp.tanh(h)


def _layer_kernel(x_ref, p_ref, w_in_f, pool_w_f, glu_w_f, wbp_f, wbs_f, w_out_f, w_ple_f,
                  pool_scale_ref, wbt_ref, wct_ref, ar_ref, ai_ref, dskip_ref, glu_b_ref,
                  lng_ref, lnb_ref, o_ref,
                  w_in_ref, pool_w_ref, glu_w_ref, wbp_ref, wbs_ref, w_out_ref, w_ple_ref,
                  halo_ref, ut_ref, utb_ref, zstate_ref, yt_ref,
                  *, batch, tt, alpha):
    step = pl.program_id(0)

    @pl.when(step < WEIGHT_STEPS)
    def _():
        col = lax.broadcasted_iota(jnp.int32, (1, IN_WIDTH), 1)
        is_value = (col < _OFF_POOL_GATE) | ((col >= _OFF_SSM_IN) & (col < _OFF_SSM_GATE))
        gate_scale = jnp.where(is_value, 1.0, 0.5).astype(_F32)

        def stage(dst_ref, src_ref, scale):
            chunk = src_ref.shape[0]
            r0 = pl.multiple_of(step * chunk, chunk)
            dst_ref[pl.ds(r0, chunk), :] = (src_ref[...] * scale).astype(_BF16)

        stage(w_in_ref, w_in_f, gate_scale)
        stage(pool_w_ref, pool_w_f, 1.0)
        stage(glu_w_ref, glu_w_f, 0.5)
        stage(wbp_ref, wbp_f, 0.5)
        stage(wbs_ref, wbs_f, 0.5)
        stage(w_out_ref, w_out_f, 1.0)
        stage(w_ple_ref, w_ple_f, 0.5)

    @pl.when(step == WEIGHT_STEPS)
    def _():
        halo_ref[:, 0:POOL_HALO, :] = jnp.zeros((batch, POOL_HALO, POOL_WIDTH), _F32)
        zstate_ref[...] = jnp.zeros_like(zstate_ref)

    @pl.when(step >= WEIGHT_STEPS)
    def _():
        _tile(x_ref, p_ref, w_in_ref, pool_w_ref, pool_scale_ref, wbt_ref, wct_ref,
              ar_ref, ai_ref, dskip_ref, glu_w_ref, glu_b_ref, wbp_ref, wbs_ref,
              w_out_ref, w_ple_ref, lng_ref, lnb_ref, o_ref,
              halo_ref, ut_ref, utb_ref, zstate_ref, yt_ref,
              tile=step - WEIGHT_STEPS, batch=batch, tt=tt, alpha=alpha)


def _tile(x_ref, p_ref, w_in_ref, pool_w_ref, pool_scale_ref, wbt_ref, wct_ref,
          ar_ref, ai_ref, dskip_ref, glu_w_ref, glu_b_ref, wbp_ref, wbs_ref,
          w_out_ref, w_ple_ref, lng_ref, lnb_ref, o_ref,
          halo_ref, ut_ref, utb_ref, zstate_ref, yt_ref,
          *, tile, batch, tt, alpha):
    rows = batch * tt
    pitch = tt + ROW_PAD

    xb = x_ref[...].reshape(rows, D_MODEL).astype(_BF16)

    def proj(off, width):
        return _dot(xb, w_in_ref[:, off:off + width])

    def expand(blk):
        cols = []
        for q in range(SCAN_PAIRS):
            j = blk * SCAN_PAIRS + q
            c = j // PAIRS_PER_LANE_TILE
            cols.append(_dot_t(utb_ref[:, c * LANES:(c + 1) * LANES], wbt_ref[j]))
        return jnp.concatenate(cols, axis=1)

    def scan(blk, bu):
        l0 = blk * BLOCK_COLS
        z = zstate_ref[:, l0:l0 + BLOCK_COLS]
        steps = []
        for t in range(tt):
            but = bu[t * batch:(t + 1) * batch, :]
            out = []
            for q in range(SCAN_PAIRS):
                j = blk * SCAN_PAIRS + q
                a_r = ar_ref[j:j + 1, :]
                a_i = ai_ref[j:j + 1, :]
                o = q * PAIR_COLS
                z_r = z[:, o:o + LANES]
                z_i = z[:, o + LANES:o + PAIR_COLS]
                out.append(a_r * z_r - a_i * z_i + but[:, o:o + LANES])
                out.append(a_r * z_i + a_i * z_r + but[:, o + LANES:o + PAIR_COLS])
            z = jnp.concatenate(out, axis=1)
            steps.append(z.astype(_BF16))
        zstate_ref[:, l0:l0 + BLOCK_COLS] = z
        return jnp.concatenate(steps, axis=0)

    def contract(blk, zb):
        n = blk // BLOCKS_PER_WC
        k0 = (blk % BLOCKS_PER_WC) * BLOCK_COLS
        return _dot_t(zb, wct_ref[n, :, k0:k0 + BLOCK_COLS])

    u = proj(_OFF_SSM_IN, SSM_WIDTH)
    pool_in = proj(_OFF_POOL_IN, POOL_WIDTH)

    for c in range(SSM_LANE_TILES):
        for b in range(batch):
            ut_ref[c, b * pitch:b * pitch + tt, :] = (
                u[b * tt:(b + 1) * tt, c * LANES:(c + 1) * LANES])
    for t in range(tt):
        for c in range(SSM_LANE_TILES):
            v = ut_ref[c, pl.ds(t, batch, stride=pitch), :]
            utb_ref[t * batch:(t + 1) * batch, c * LANES:(c + 1) * LANES] = v.astype(_BF16)

    halo_ref[:, POOL_HALO:POOL_HALO + tt, :] = pool_in.reshape(batch, tt, POOL_WIDTH)
    bu0 = expand(0)
    bu1 = expand(1)

    t_glob = tile * tt + lax.broadcasted_iota(jnp.int32, (1, tt, POOL_GROUP), 1)
    d_blocks = []
    for gi, win in enumerate(POOL_WINDOWS):
        c0 = gi * POOL_GROUP
        s = halo_ref[:, :, c0:c0 + POOL_GROUP]
        cur = s[:, POOL_HALO:, :]
        k = 1
        while k < win:
            s = s + pltpu.roll(s, k, axis=1)
            k *= 2
        inv_cnt = 1.0 / jnp.minimum(t_glob + 1, win).astype(_F32)
        d = s[:, POOL_HALO:, :] * inv_cnt - cur
        d_blocks.append(d.reshape(rows, POOL_GROUP).astype(_BF16))
    halo_ref[:, 0:POOL_HALO, :] = halo_ref[:, tt:tt + POOL_HALO, :]

    h_pool = proj(_OFF_POOL_GATE, POOL_WIDTH)
    zb0 = scan(0, bu0)
    bu2 = expand(2)
    bu3 = expand(3)
    pooled = [_dot(d_blocks[gi], pool_w_ref[gi * POOL_GROUP:(gi + 1) * POOL_GROUP, :])
              * pool_scale_ref[:, gi * POOL_GROUP:(gi + 1) * POOL_GROUP]
              for gi in range(len(POOL_WINDOWS))]
    t_gpool = jnp.tanh(proj(_OFF_G_POOL, D_MODEL))
    zb1 = scan(1, bu1)
    y_tb0 = contract(0, zb0)
    y_pool = (jnp.concatenate(pooled, axis=1) * _half_silu(h_pool)).astype(_BF16)
    half_bp = _dot(y_pool, wbp_ref[...])
    zb2 = scan(2, bu2)
    h_ssm = proj(_OFF_SSM_GATE, SSM_WIDTH)
    y_tb0 = y_tb0 + contract(1, zb1)
    merged_pool = half_bp + t_gpool * half_bp
    zb3 = scan(3, bu3)
    t_gssm = jnp.tanh(proj(_OFF_G_SSM, D_MODEL))
    y_tb1 = contract(2, zb2) + contract(3, zb3)

    for n, y_tb in enumerate((y_tb0, y_tb1)):
        for t in range(tt):
            for hh in range(MXU_DIM // LANES):
                c = n * (MXU_DIM // LANES) + hh
                yt_ref[c, pl.ds(t, batch, stride=pitch), :] = (
                    y_tb[t * batch:(t + 1) * batch, hh * LANES:(hh + 1) * LANES])
    y_cols = []
    for c in range(SSM_LANE_TILES):
        y_cols.append(jnp.concatenate(
            [yt_ref[c, b * pitch:b * pitch + tt, :] for b in range(batch)], axis=0))
    y = jnp.concatenate(y_cols, axis=1) + dskip_ref[...] * u
    half_y = 0.5 * y
    g = half_y + half_y * jnp.tanh(y * (_GELU_C + (_GELU_C * 0.044715) * (y * y)))
    t_ple = jnp.tanh(proj(_OFF_PLE_GATE, D_MODEL))
    half_q = _dot(g.astype(_BF16), glu_w_ref[...]) + 0.5 * glu_b_ref[...]
    pb = p_ref[...].reshape(rows, PLE_DIM).astype(_BF16)
    half_ple = _dot(pb, w_ple_ref[...])
    half_g = 0.5 * g
    y_ssm = ((half_g + half_g * jnp.tanh(half_q)) * _half_silu(h_ssm)).astype(_BF16)
    ple = half_ple + t_ple * half_ple

    bsplit = batch // ROW_SPLITS
    rsplit = rows // ROW_SPLITS
    for r in range(ROW_SPLITS):
        rs = slice(r * rsplit, (r + 1) * rsplit)
        bs = slice(r * bsplit, (r + 1) * bsplit)
        half_bs = _dot(y_ssm[rs], wbs_ref[...])
        merged = merged_pool[rs] + half_bs + t_gssm[rs] * half_bs
        mix = _dot(merged.astype(_BF16), w_out_ref[...])
        h = alpha * x_ref[bs].reshape(rsplit, D_MODEL) + mix + ple[rs]
        mu = jnp.mean(h, axis=-1, keepdims=True)
        hc = h - mu
        var = jnp.mean(hc * hc, axis=-1, keepdims=True)
        out = hc * lax.rsqrt(var + LN_EPS) * lng_ref[...] + lnb_ref[...]
        o_ref[bs] = out.reshape(bsplit, tt, D_MODEL).astype(o_ref.dtype)


def _cmul(ar, ai, br, bi):
    return ar * br - ai * bi, ar * bi + ai * br


def _ssm_operands(a_re, a_im, log_dt, b_re, b_im, c_re, c_im):
    f32 = _F32
    dt = jnp.exp(log_dt.astype(f32))[:, None]
    lr = a_re.astype(f32)
    li = a_im.astype(f32)
    mag = jnp.exp(lr * dt)
    abar_r = mag * jnp.cos(li * dt)
    abar_i = mag * jnp.sin(li * dt)
    den = lr * lr + li * li
    zr, zi = _cmul(abar_r - 1.0, abar_i, lr, -li)
    zr = zr / den
    zi = zi / den
    bbar_r, bbar_i = _cmul(zr[..., None], zi[..., None], b_re.astype(f32), b_im.astype(f32))

    ar = abar_r.reshape(N_PAIRS, LANES)
    ai = abar_i.reshape(N_PAIRS, LANES)

    pair_shape = (N_PAIRS, GROUPS_PER_PAIR, SSM_STATE, SSM_GROUP_CH)
    bb = jnp.stack([bbar_r.reshape(pair_shape), bbar_i.reshape(pair_shape)], axis=1)
    jj = jnp.arange(N_PAIRS)[:, None, None]
    qq = jnp.arange(GROUPS_PER_PAIR)[None, :, None]
    gg = jnp.arange(GROUPS_PER_LANE_TILE)[None, None, :]
    sel_b = (gg == (jj % PAIRS_PER_LANE_TILE) * GROUPS_PER_PAIR + qq).astype(f32)
    wbt = bb[:, :, :, :, None, :] * sel_b[:, None, :, None, :, None]
    wbt = wbt.reshape(N_PAIRS, PAIR_COLS, LANES)

    pairs_per_wc = N_PAIRS // WC_TILES
    groups_per_wc = SSM_GROUPS // WC_TILES
    out_shape = (WC_TILES, groups_per_wc, SSM_GROUP_CH, SSM_STATE)
    cc = jnp.stack([c_re.astype(f32).reshape(out_shape),
                    -c_im.astype(f32).reshape(out_shape)], axis=3)
    g16 = jnp.arange(groups_per_wc)[:, None, None]
    ll = jnp.arange(pairs_per_wc)[None, :, None]
    q_last = jnp.arange(GROUPS_PER_PAIR)[None, None, :]
    sel_c = (g16 == ll * GROUPS_PER_PAIR + q_last).astype(f32)
    wct = cc[:, :, :, None, :, None, :] * sel_c[None, :, None, :, None, :, None]
    wct = wct.reshape(WC_TILES, MXU_DIM, pairs_per_wc * PAIR_COLS)
    return ar, ai, wbt.astype(_BF16), wct.astype(_BF16)


def _const_spec(shape):
    zeros = (0,) * len(shape)
    return pl.BlockSpec(shape, lambda s, _z=zeros: _z, pipeline_mode=pl.Buffered(1))


def _chunk_spec(shape):
    chunk = shape[0] // WEIGHT_STEPS
    assert chunk * WEIGHT_STEPS == shape[0] and chunk % BF16_SUBLANES == 0
    return pl.BlockSpec((chunk, shape[1]), lambda s: (jnp.minimum(s, WEIGHT_STEPS - 1), 0))


def _layer(x, p, w_in, pool_w, pool_scale, ssm, d_skip, glu_w, glu_b,
           w_branch_pool, w_branch_ssm, w_out, w_ple, ln_g, ln_b, alpha):
    batch, seq, d_model = x.shape
    assert d_model == D_MODEL and w_in.shape == (D_MODEL, IN_WIDTH)
    tt = TIME_TILE
    assert seq % tt == 0 and tt % SUBLANES == 0 and tt >= POOL_HALO
    assert batch % (SUBLANES * ROW_SPLITS) == 0
    rows = batch * tt
    pitch = tt + ROW_PAD
    ar, ai, wbt, wct = ssm
    row = lambda v: v.astype(_F32).reshape(1, -1)
    f32_weights = (w_in, pool_w.reshape(POOL_WIDTH, POOL_GROUP), glu_w, w_branch_pool,
                   w_branch_ssm, w_out, w_ple)
    small = (row(pool_scale), wbt, wct, ar, ai, row(d_skip), row(glu_b), row(ln_g), row(ln_b))
    time_tile = lambda s: (0, jnp.maximum(s - WEIGHT_STEPS, 0), 0)
    in_specs = (
        [pl.BlockSpec((batch, tt, D_MODEL), time_tile),
         pl.BlockSpec((batch, tt, PLE_DIM), time_tile)]
        + [_chunk_spec(w.shape) for w in f32_weights]
        + [_const_spec(op.shape) for op in small])
    kernel = functools.partial(_layer_kernel, batch=batch, tt=tt, alpha=alpha)
    return pl.pallas_call(
        kernel,
        grid=(WEIGHT_STEPS + seq // tt,),
        in_specs=in_specs,
        out_specs=pl.BlockSpec((batch, tt, D_MODEL), time_tile),
        out_shape=jax.ShapeDtypeStruct(x.shape, x.dtype),
        scratch_shapes=[pltpu.VMEM(w.shape, _BF16) for w in f32_weights] + [
            pltpu.VMEM((batch, POOL_HALO + tt, POOL_WIDTH), _F32),
            pltpu.VMEM((SSM_LANE_TILES, batch * pitch, LANES), _F32),
            pltpu.VMEM((rows, SSM_WIDTH), _BF16),
            pltpu.VMEM((batch, STATE_COLS), _F32),
            pltpu.VMEM((SSM_LANE_TILES, batch * pitch, LANES), _F32),
        ],
        compiler_params=pltpu.CompilerParams(
            dimension_semantics=("arbitrary",),
            vmem_limit_bytes=VMEM_LIMIT_BYTES,
        ),
        name="hybrid_pool_s5_layer",
    )(x, p, *f32_weights, *small)


def kernel(x, p, w_in, pool_w, pool_scale, ssm_a_re, ssm_a_im, ssm_log_dt, ssm_b_re, ssm_b_im, ssm_c_re, ssm_c_im, ssm_d, glu_w, glu_b, w_branch_pool, w_branch_ssm, w_out, w_ple, ln_g, ln_b):
    depth = w_in.shape[0]
    alpha = (2.0 * depth) ** 0.25
    for i in range(depth):
        ssm = _ssm_operands(ssm_a_re[i], ssm_a_im[i], ssm_log_dt[i], ssm_b_re[i],
                            ssm_b_im[i], ssm_c_re[i], ssm_c_im[i])
        x = _layer(x, p[i], w_in[i], pool_w[i], pool_scale[i], ssm, ssm_d[i], glu_w[i],
                   glu_b[i], w_branch_pool[i], w_branch_ssm[i], w_out[i], w_ple[i],
                   ln_g[i], ln_b[i], alpha)
    return x
```

```python
import functools
import math

import jax
import jax.numpy as jnp
from jax import lax
from jax.experimental import pallas as pl
from jax.experimental.pallas import tpu as pltpu

D_MODEL = 1024
PLE_DIM = 256
POOL_WIDTH = D_MODEL
POOL_WINDOWS = (2, 4, 8, 16)
POOL_GROUP = POOL_WIDTH // len(POOL_WINDOWS)
POOL_HALO = max(POOL_WINDOWS)
SSM_WIDTH = D_MODEL // 2
SSM_GROUP_CH = 16
SSM_GROUPS = SSM_WIDTH // SSM_GROUP_CH
SSM_STATE = 64
LN_EPS = 1e-5

_OFF_POOL_IN = 0
_OFF_POOL_GATE = POOL_WIDTH
_OFF_SSM_IN = 2 * POOL_WIDTH
_OFF_SSM_GATE = 2 * POOL_WIDTH + SSM_WIDTH
_OFF_G_POOL = 2 * POOL_WIDTH + 2 * SSM_WIDTH
_OFF_G_SSM = _OFF_G_POOL + D_MODEL
_OFF_PLE_GATE = _OFF_G_SSM + D_MODEL
IN_WIDTH = _OFF_PLE_GATE + D_MODEL

LANES = 128
SUBLANES = 8
MXU_DIM = 256
VMEM_LIMIT_BYTES = 56 * 1024 * 1024

GROUPS_PER_PAIR = LANES // SSM_STATE
N_PAIRS = SSM_GROUPS // GROUPS_PER_PAIR
PAIR_COLS = 2 * LANES
GROUPS_PER_LANE_TILE = LANES // SSM_GROUP_CH
PAIRS_PER_LANE_TILE = GROUPS_PER_LANE_TILE // GROUPS_PER_PAIR
SSM_LANE_TILES = SSM_WIDTH // LANES
BLOCK_COLS = PAIRS_PER_LANE_TILE * PAIR_COLS
STATE_COLS = SSM_LANE_TILES * BLOCK_COLS
TIME_FOLD = 2
FOLD_COLS = TIME_FOLD * LANES

TIME_TILE = 32
ROW_PAD = 8
ROW_SPLITS = 2

_BF16 = jnp.bfloat16
_F32 = jnp.float32
_GELU_C = math.sqrt(2.0 / math.pi)
_CONTRACT_LAST = (((1,), (1,)), ((), ()))
_EXACT = lax.Precision.HIGHEST


def _dot(a, b):
    return jnp.dot(a, b, preferred_element_type=_F32)


def _dot_t(a, b_t):
    return lax.dot_general(a, b_t, _CONTRACT_LAST, preferred_element_type=_F32)


def _half_silu(h):
    return h + h * jnp.tanh(h)


def _layer_kernel(x_ref, p_ref, w_in_ref, pool_w_ref, pool_scale_ref, wbt_ref, wct_ref, wdt_ref,
                  ar_ref, ai_ref, dskip_ref, glu_w_ref, glu_b_ref, wbp_ref, wbs_ref,
                  w_out_ref, w_ple_ref, lng_ref, lnb_ref, o_ref,
                  halo_ref, ut_ref, utb_ref, zstate_ref, yt_ref,
                  *, batch, tt, alpha):
    rows = batch * tt
    folds = tt // TIME_FOLD
    pitch = tt + ROW_PAD
    step = pl.program_id(0)

    @pl.when(step == 0)
    def _():
        halo_ref[:, 0:POOL_HALO, :] = jnp.zeros((batch, POOL_HALO, POOL_WIDTH), _F32)
        zstate_ref[...] = jnp.zeros_like(zstate_ref)

    xb = x_ref[...].reshape(rows, D_MODEL).astype(_BF16)

    def proj(off, width):
        return _dot(xb, w_in_ref[:, off:off + width])

    def fold_inputs(c):
        return utb_ref[:, c * FOLD_COLS:(c + 1) * FOLD_COLS]

    def expand(c):
        cols = []
        for q in range(PAIRS_PER_LANE_TILE):
            j = c * PAIRS_PER_LANE_TILE + q
            cols.append(_dot_t(fold_inputs(c), wbt_ref[j]))
        return jnp.concatenate(cols, axis=1)

    def scan(c, bu):
        l0 = c * BLOCK_COLS
        z = zstate_ref[:, l0:l0 + BLOCK_COLS]
        prev = []
        for k in range(folds):
            prev.append(z.astype(_BF16))
            buk = bu[k * batch:(k + 1) * batch, :]
            out = []
            for q in range(PAIRS_PER_LANE_TILE):
                j = c * PAIRS_PER_LANE_TILE + q
                a_r = ar_ref[j:j + 1, :]
                a_i = ai_ref[j:j + 1, :]
                o = q * PAIR_COLS
                z_r = z[:, o:o + LANES]
                z_i = z[:, o + LANES:o + PAIR_COLS]
                out.append(a_r * z_r - a_i * z_i + buk[:, o:o + LANES])
                out.append(a_r * z_i + a_i * z_r + buk[:, o + LANES:o + PAIR_COLS])
            z = jnp.concatenate(out, axis=1)
        zstate_ref[:, l0:l0 + BLOCK_COLS] = z
        return jnp.concatenate(prev, axis=0)

    def contract(c, s_prev):
        return _dot_t(s_prev, wct_ref[c]) + _dot_t(fold_inputs(c), wdt_ref[c])

    u = proj(_OFF_SSM_IN, SSM_WIDTH)
    pool_in = proj(_OFF_POOL_IN, POOL_WIDTH)

    for c in range(SSM_LANE_TILES):
        for b in range(batch):
            ut_ref[c, b * pitch:b * pitch + tt, :] = (
                u[b * tt:(b + 1) * tt, c * LANES:(c + 1) * LANES])
    for t in range(tt):
        k, e = divmod(t, TIME_FOLD)
        for c in range(SSM_LANE_TILES):
            v = ut_ref[c, pl.ds(t, batch, stride=pitch), :]
            l0 = c * FOLD_COLS + e * LANES
            utb_ref[k * batch:(k + 1) * batch, l0:l0 + LANES] = v.astype(_BF16)

    halo_ref[:, POOL_HALO:POOL_HALO + tt, :] = pool_in.reshape(batch, tt, POOL_WIDTH)
    bu0 = expand(0)
    bu1 = expand(1)

    t_glob = step * tt + lax.broadcasted_iota(jnp.int32, (1, tt, POOL_GROUP), 1)
    d_blocks = []
    for gi, win in enumerate(POOL_WINDOWS):
        c0 = gi * POOL_GROUP
        s = halo_ref[:, :, c0:c0 + POOL_GROUP]
        cur = s[:, POOL_HALO:, :]
        k = 1
        while k < win:
            s = s + pltpu.roll(s, k, axis=1)
            k *= 2
        inv_cnt = 1.0 / jnp.minimum(t_glob + 1, win).astype(_F32)
        d = s[:, POOL_HALO:, :] * inv_cnt - cur
        d_blocks.append(d.reshape(rows, POOL_GROUP).astype(_BF16))
    halo_ref[:, 0:POOL_HALO, :] = halo_ref[:, tt:tt + POOL_HALO, :]

    h_pool = proj(_OFF_POOL_GATE, POOL_WIDTH)
    sp0 = scan(0, bu0)
    bu2 = expand(2)
    bu3 = expand(3)
    pooled = [_dot(d_blocks[gi], pool_w_ref[gi])
              * pool_scale_ref[:, gi * POOL_GROUP:(gi + 1) * POOL_GROUP]
              for gi in range(len(POOL_WINDOWS))]
    t_gpool = jnp.tanh(proj(_OFF_G_POOL, D_MODEL))
    sp1 = scan(1, bu1)
    y_fold = [contract(0, sp0)]
    y_pool = (jnp.concatenate(pooled, axis=1) * _half_silu(h_pool)).astype(_BF16)
    half_bp = _dot(y_pool, wbp_ref[...])
    sp2 = scan(2, bu2)
    h_ssm = proj(_OFF_SSM_GATE, SSM_WIDTH)
    y_fold.append(contract(1, sp1))
    merged_pool = half_bp + t_gpool * half_bp
    sp3 = scan(3, bu3)
    t_gssm = jnp.tanh(proj(_OFF_G_SSM, D_MODEL))
    y_fold.append(contract(2, sp2))
    y_fold.append(contract(3, sp3))

    for c in range(SSM_LANE_TILES):
        for t in range(tt):
            k, e = divmod(t, TIME_FOLD)
            yt_ref[c, pl.ds(t, batch, stride=pitch), :] = (
                y_fold[c][k * batch:(k + 1) * batch, e * LANES:(e + 1) * LANES])
    y_cols = []
    for c in range(SSM_LANE_TILES):
        y_cols.append(jnp.concatenate(
            [yt_ref[c, b * pitch:b * pitch + tt, :] for b in range(batch)], axis=0))
    y = jnp.concatenate(y_cols, axis=1) + dskip_ref[...] * u
    half_y = 0.5 * y
    g = half_y + half_y * jnp.tanh(y * (_GELU_C + (_GELU_C * 0.044715) * (y * y)))
    t_ple = jnp.tanh(proj(_OFF_PLE_GATE, D_MODEL))
    half_q = _dot(g.astype(_BF16), glu_w_ref[...]) + glu_b_ref[...]
    pb = p_ref[...].reshape(rows, PLE_DIM).astype(_BF16)
    half_ple = _dot(pb, w_ple_ref[...])
    half_g = 0.5 * g
    y_ssm = ((half_g + half_g * jnp.tanh(half_q)) * _half_silu(h_ssm)).astype(_BF16)
    ple = half_ple + t_ple * half_ple

    bsplit = batch // ROW_SPLITS
    rsplit = rows // ROW_SPLITS
    for r in range(ROW_SPLITS):
        rs = slice(r * rsplit, (r + 1) * rsplit)
        bs = slice(r * bsplit, (r + 1) * bsplit)
        half_bs = _dot(y_ssm[rs], wbs_ref[...])
        merged = merged_pool[rs] + half_bs + t_gssm[rs] * half_bs
        mix = _dot(merged.astype(_BF16), w_out_ref[...])
        h = alpha * x_ref[bs].reshape(rsplit, D_MODEL) + mix + ple[rs]
        mu = jnp.mean(h, axis=-1, keepdims=True)
        hc = h - mu
        var = jnp.mean(hc * hc, axis=-1, keepdims=True)
        out = hc * lax.rsqrt(var + LN_EPS) * lng_ref[...] + lnb_ref[...]
        o_ref[bs] = out.reshape(bsplit, tt, D_MODEL).astype(o_ref.dtype)


def _cmul(ar, ai, br, bi):
    return ar * br - ai * bi, ar * bi + ai * br


def _spread_lanes(x2d, period):
    onehot = (jnp.arange(LANES)[None, :] % period == jnp.arange(period)[:, None]).astype(_F32)
    return jnp.dot(x2d, onehot, precision=_EXACT)


def _ssm_operands(a_re, a_im, log_dt, b_re, b_im, c_re, c_im):
    f32 = _F32
    dt = jnp.exp(log_dt.astype(f32))[:, None]
    lr = a_re.astype(f32)
    li = a_im.astype(f32)
    mag = jnp.exp(lr * dt)
    abar_r = mag * jnp.cos(li * dt)
    abar_i = mag * jnp.sin(li * dt)
    den = lr * lr + li * li
    zr, zi = _cmul(abar_r - 1.0, abar_i, lr, -li)
    zr = zr / den
    zi = zi / den
    bbar_r, bbar_i = _cmul(zr[..., None], zi[..., None], b_re.astype(f32), b_im.astype(f32))

    a2_r, a2_i = _cmul(abar_r, abar_i, abar_r, abar_i)
    ab_r, ab_i = _cmul(abar_r[..., None], abar_i[..., None], bbar_r, bbar_i)
    cr = c_re.astype(f32)
    ci = c_im.astype(f32)
    ca_r, ca_i = _cmul(cr, ci, abar_r[:, None, :], abar_i[:, None, :])
    ca2_r, ca2_i = _cmul(cr, ci, a2_r[:, None, :], a2_i[:, None, :])

    def re_prod(xr, xi, yr, yi):
        return (jnp.einsum("ghp,gpk->ghk", xr, yr, precision=_EXACT)
                - jnp.einsum("ghp,gpk->ghk", xi, yi, precision=_EXACT))

    cb = re_prod(cr, ci, bbar_r, bbar_i)
    cab = re_prod(ca_r, ca_i, bbar_r, bbar_i)

    ar2 = a2_r.reshape(N_PAIRS, LANES)
    ai2 = a2_i.reshape(N_PAIRS, LANES)

    n_state_rows = N_PAIRS * PAIR_COLS
    row = jnp.arange(n_state_rows)[:, None]
    lane_group = jnp.arange(LANES)[None, :] // SSM_GROUP_CH
    src_group = ((row // PAIR_COLS) % PAIRS_PER_LANE_TILE) * GROUPS_PER_PAIR + (row // SSM_STATE) % 2
    keep_b = lane_group == src_group

    def state_rows(xr, xi):
        shape = (N_PAIRS, GROUPS_PER_PAIR * SSM_STATE, SSM_GROUP_CH)
        x = jnp.stack([xr.reshape(shape), xi.reshape(shape)], axis=1).reshape(
            n_state_rows, SSM_GROUP_CH)
        return jnp.where(keep_b, _spread_lanes(x, SSM_GROUP_CH), 0.0)

    wbt = jnp.concatenate([state_rows(ab_r, ab_i), state_rows(bbar_r, bbar_i)], axis=1)
    wbt = wbt.reshape(N_PAIRS, PAIR_COLS, FOLD_COLS)

    ch = jnp.arange(SSM_WIDTH)[:, None]
    kcol = jnp.arange(BLOCK_COLS)[None, :]
    dst_group = (ch // SSM_GROUP_CH) % GROUPS_PER_LANE_TILE
    col_group = (kcol // PAIR_COLS) * GROUPS_PER_PAIR + (kcol // SSM_STATE) % 2
    keep_c = dst_group == col_group

    def out_rows(xr, xi):
        x2r = xr.reshape(SSM_WIDTH, SSM_STATE)
        x2i = -xi.reshape(SSM_WIDTH, SSM_STATE)
        pair = jnp.concatenate([x2r, x2r, x2i, x2i], axis=1)
        full = jnp.concatenate([pair] * PAIRS_PER_LANE_TILE, axis=1)
        return jnp.where(keep_c, full, 0.0).reshape(SSM_LANE_TILES, LANES, BLOCK_COLS)

    wct = jnp.concatenate([out_rows(ca_r, ca_i), out_rows(ca2_r, ca2_i)], axis=1)

    keep_d = lane_group == dst_group

    def direct(m):
        x = _spread_lanes(m.reshape(SSM_WIDTH, SSM_GROUP_CH), SSM_GROUP_CH)
        return jnp.where(keep_d, x, 0.0).reshape(SSM_LANE_TILES, LANES, LANES)

    d_cb = direct(cb)
    wdt = jnp.concatenate([
        jnp.concatenate([d_cb, jnp.zeros_like(d_cb)], axis=2),
        jnp.concatenate([direct(cab), d_cb], axis=2)], axis=1)
    return ar2, ai2, wbt.astype(_BF16), wct.astype(_BF16), wdt.astype(_BF16)


def _const_spec(shape):
    zeros = (0,) * len(shape)
    return pl.BlockSpec(shape, lambda i, _z=zeros: _z, pipeline_mode=pl.Buffered(1))


def _layer(x, p, w_in, pool_w, pool_scale, ssm, d_skip, glu_w, glu_b,
           w_branch_pool, w_branch_ssm, w_out, w_ple, ln_g, ln_b, alpha):
    batch, seq, d_model = x.shape
    assert d_model == D_MODEL and w_in.shape == (D_MODEL, IN_WIDTH)
    tt = TIME_TILE
    assert seq % tt == 0 and tt % SUBLANES == 0 and tt >= POOL_HALO and tt % TIME_FOLD == 0
    assert batch % (SUBLANES * ROW_SPLITS) == 0
    rows = batch * tt
    pitch = tt + ROW_PAD
    ar2, ai2, wbt, wct, wdt = ssm
    row = lambda v: v.astype(_F32).reshape(1, -1)
    col = jnp.arange(IN_WIDTH)
    is_value = (col < _OFF_POOL_GATE) | ((col >= _OFF_SSM_IN) & (col < _OFF_SSM_GATE))
    gate_scale = jnp.where(is_value, 1.0, 0.5).astype(_F32)
    operands = (
        x, p,
        (w_in * gate_scale).astype(_BF16), pool_w.astype(_BF16), row(pool_scale),
        wbt, wct, wdt, ar2, ai2, row(d_skip),
        (0.5 * glu_w).astype(_BF16), row(0.5 * glu_b),
        (0.5 * w_branch_pool).astype(_BF16), (0.5 * w_branch_ssm).astype(_BF16),
        w_out.astype(_BF16), (0.5 * w_ple).astype(_BF16), row(ln_g), row(ln_b),
    )
    in_specs = [
        pl.BlockSpec((batch, tt, D_MODEL), lambda i: (0, i, 0)),
        pl.BlockSpec((batch, tt, PLE_DIM), lambda i: (0, i, 0)),
    ] + [_const_spec(op.shape) for op in operands[2:]]
    kernel = functools.partial(_layer_kernel, batch=batch, tt=tt, alpha=alpha)
    return pl.pallas_call(
        kernel,
        grid=(seq // tt,),
        in_specs=in_specs,
        out_specs=pl.BlockSpec((batch, tt, D_MODEL), lambda i: (0, i, 0)),
        out_shape=jax.ShapeDtypeStruct(x.shape, x.dtype),
        scratch_shapes=[
            pltpu.VMEM((batch, POOL_HALO + tt, POOL_WIDTH), _F32),
            pltpu.VMEM((SSM_LANE_TILES, batch * pitch, LANES), _F32),
            pltpu.VMEM((rows // TIME_FOLD, SSM_LANE_TILES * FOLD_COLS), _BF16),
            pltpu.VMEM((batch, STATE_COLS), _F32),
            pltpu.VMEM((SSM_LANE_TILES, batch * pitch, LANES), _F32),
        ],
        compiler_params=pltpu.CompilerParams(
            dimension_semantics=("arbitrary",),
            vmem_limit_bytes=VMEM_LIMIT_BYTES,
        ),
        name="hybrid_pool_s5_layer",
    )(*operands)


def kernel(x, p, w_in, pool_w, pool_scale, ssm_a_re, ssm_a_im, ssm_log_dt, ssm_b_re, ssm_b_im, ssm_c_re, ssm_c_im, ssm_d, glu_w, glu_b, w_branch_pool, w_branch_ssm, w_out, w_ple, ln_g, ln_b):
    depth = w_in.shape[0]
    alpha = (2.0 * depth) ** 0.25
    for i in range(depth):
        ssm = _ssm_operands(ssm_a_re[i], ssm_a_im[i], ssm_log_dt[i], ssm_b_re[i],
                            ssm_b_im[i], ssm_c_re[i], ssm_c_im[i])
        x = _layer(x, p[i], w_in[i], pool_w[i], pool_scale[i], ssm, ssm_d[i], glu_w[i],
                   glu_b[i], w_branch_pool[i], w_branch_ssm[i], w_out[i], w_ple[i],
                   ln_g[i], ln_b[i], alpha)
    return x
```

```python
import functools
import math

import jax
import jax.numpy as jnp
from jax import lax
from jax.experimental import pallas as pl
from jax.experimental.pallas import tpu as pltpu

D_MODEL = 1024
PLE_DIM = 256
POOL_WIDTH = D_MODEL
POOL_WINDOWS = (2, 4, 8, 16)
POOL_GROUP = POOL_WIDTH // len(POOL_WINDOWS)
POOL_HALO = max(POOL_WINDOWS)
SSM_WIDTH = D_MODEL // 2
SSM_GROUP_CH = 16
SSM_GROUPS = SSM_WIDTH // SSM_GROUP_CH
SSM_STATE = 64
LN_EPS = 1e-5

_OFF_POOL_IN = 0
_OFF_POOL_GATE = POOL_WIDTH
_OFF_SSM_IN = 2 * POOL_WIDTH
_OFF_SSM_GATE = 2 * POOL_WIDTH + SSM_WIDTH
_OFF_G_POOL = 2 * POOL_WIDTH + 2 * SSM_WIDTH
_OFF_G_SSM = _OFF_G_POOL + D_MODEL
_OFF_PLE_GATE = _OFF_G_SSM + D_MODEL
IN_WIDTH = _OFF_PLE_GATE + D_MODEL

LANES = 128
SUBLANES = 8
MXU_DIM = 256
VMEM_LIMIT_BYTES = 60 * 1024 * 1024

GROUPS_PER_PAIR = LANES // SSM_STATE
N_PAIRS = SSM_GROUPS // GROUPS_PER_PAIR
PAIR_COLS = 2 * LANES
GROUPS_PER_LANE_TILE = LANES // SSM_GROUP_CH
PAIRS_PER_LANE_TILE = GROUPS_PER_LANE_TILE // GROUPS_PER_PAIR
SSM_LANE_TILES = SSM_WIDTH // LANES
BLOCK_COLS = PAIRS_PER_LANE_TILE * PAIR_COLS
STATE_COLS = SSM_LANE_TILES * BLOCK_COLS
TIME_FOLD = 2
FOLD_COLS = TIME_FOLD * LANES

TIME_TILE = 32
TILES_PER_STEP = 1
ROW_PAD = 8
ROW_SPLITS = 2

_BF16 = jnp.bfloat16
_F32 = jnp.float32
_GELU_C = math.sqrt(2.0 / math.pi)
_CONTRACT_LAST = (((1,), (1,)), ((), ()))
_EXACT = lax.Precision.HIGHEST


def _dot(a, b):
    return jnp.dot(a, b, preferred_element_type=_F32)


def _dot_t(a, b_t):
    return lax.dot_general(a, b_t, _CONTRACT_LAST, preferred_element_type=_F32)


def _half_silu(h):
    return h + h * jnp.tanh(h)


def _layer_kernel(x_ref, p_ref, w_in_ref, pool_w_ref, pool_scale_ref, wbt_ref, wct_ref, wdt_ref,
                  ar_ref, ai_ref, dskip_ref, glu_w_ref, glu_b_ref, wbp_ref, wbs_ref,
                  w_out_ref, w_ple_ref, lng_ref, lnb_ref, o_ref,
                  halo_ref, ut_ref, utb_ref, zstate_ref, yt_ref, merged_ref, base_ref,
                  *, batch, tt, alpha):
    rows = batch * tt
    step = pl.program_id(0)
    last = pl.num_programs(0) - 1

    @pl.when(step == 0)
    def _():
        halo_ref[:, 0:POOL_HALO, :] = jnp.zeros((batch, POOL_HALO, POOL_WIDTH), _F32)
        zstate_ref[...] = jnp.zeros_like(zstate_ref)
        merged_ref[...] = jnp.zeros_like(merged_ref)
        base_ref[...] = jnp.zeros_like(base_ref)

    def finish_previous():
        h = base_ref[...] + _dot(merged_ref[...], w_out_ref[...])
        mu = jnp.mean(h, axis=-1, keepdims=True)
        hc = h - mu
        var = jnp.mean(hc * hc, axis=-1, keepdims=True)
        out = hc * lax.rsqrt(var + LN_EPS) * lng_ref[...] + lnb_ref[...]
        o_ref[...] = out.reshape(batch, tt, D_MODEL).astype(o_ref.dtype)
        return out

    @pl.when(step < last)
    def _():
        out_prev = finish_previous()
        bits = lax.bitcast_convert_type(out_prev, jnp.int32)
        folded = bits[:, 0:LANES]
        for c in range(1, D_MODEL // LANES):
            folded = folded | bits[:, c * LANES:(c + 1) * LANES]
        sixteen = jnp.int32(16)
        zero = lax.shift_right_logical(lax.shift_right_logical(folded, sixteen), sixteen)
        _tile(x_ref, p_ref, w_in_ref, pool_w_ref, pool_scale_ref, wbt_ref, wct_ref, wdt_ref,
              ar_ref, ai_ref, dskip_ref, glu_w_ref, glu_b_ref, wbp_ref, wbs_ref,
              w_ple_ref, halo_ref, ut_ref, utb_ref, zstate_ref, yt_ref, merged_ref, base_ref,
              zero.astype(_F32).astype(_BF16),
              tile=step, batch=batch, tt=tt, alpha=alpha)

    @pl.when(step == last)
    def _():
        finish_previous()


def _tile(x_ref, p_ref, w_in_ref, pool_w_ref, pool_scale_ref, wbt_ref, wct_ref, wdt_ref,
          ar_ref, ai_ref, dskip_ref, glu_w_ref, glu_b_ref, wbp_ref, wbs_ref,
          w_ple_ref, halo_ref, ut_ref, utb_ref, zstate_ref, yt_ref, merged_ref, base_ref,
          order_zero, *, tile, batch, tt, alpha):
    rows = batch * tt
    folds = tt // TIME_FOLD
    pitch = tt + ROW_PAD

    xb = x_ref[...].reshape(rows, D_MODEL).astype(_BF16)

    def proj(off, width):
        return _dot(xb, w_in_ref[:, off:off + width])

    def fold_inputs(c):
        return utb_ref[:, c * FOLD_COLS:(c + 1) * FOLD_COLS]

    def expand(c):
        cols = []
        for q in range(PAIRS_PER_LANE_TILE):
            j = c * PAIRS_PER_LANE_TILE + q
            cols.append(_dot_t(fold_inputs(c), wbt_ref[j]))
        return jnp.concatenate(cols, axis=1)

    def scan(c, bu):
        l0 = c * BLOCK_COLS
        z = zstate_ref[:, l0:l0 + BLOCK_COLS]
        prev = []
        for k in range(folds):
            prev.append(z.astype(_BF16))
            buk = bu[k * batch:(k + 1) * batch, :]
            out = []
            for q in range(PAIRS_PER_LANE_TILE):
                j = c * PAIRS_PER_LANE_TILE + q
                a_r = ar_ref[j:j + 1, :]
                a_i = ai_ref[j:j + 1, :]
                o = q * PAIR_COLS
                z_r = z[:, o:o + LANES]
                z_i = z[:, o + LANES:o + PAIR_COLS]
                out.append(a_r * z_r - a_i * z_i + buk[:, o:o + LANES])
                out.append(a_r * z_i + a_i * z_r + buk[:, o + LANES:o + PAIR_COLS])
            z = jnp.concatenate(out, axis=1)
        zstate_ref[:, l0:l0 + BLOCK_COLS] = z
        return jnp.concatenate(prev, axis=0)

    def contract(c, s_prev):
        return _dot_t(s_prev, wct_ref[c]) + _dot_t(fold_inputs(c), wdt_ref[c])

    u = proj(_OFF_SSM_IN, SSM_WIDTH)
    pool_in = proj(_OFF_POOL_IN, POOL_WIDTH)

    for c in range(SSM_LANE_TILES):
        for b in range(batch):
            ut_ref[c, b * pitch:b * pitch + tt, :] = (
                u[b * tt:(b + 1) * tt, c * LANES:(c + 1) * LANES])
    for t in range(tt):
        k, e = divmod(t, TIME_FOLD)
        for c in range(SSM_LANE_TILES):
            v = ut_ref[c, pl.ds(t, batch, stride=pitch), :]
            l0 = c * FOLD_COLS + e * LANES
            utb_ref[k * batch:(k + 1) * batch, l0:l0 + LANES] = v.astype(_BF16)

    halo_ref[:, POOL_HALO:POOL_HALO + tt, :] = pool_in.reshape(batch, tt, POOL_WIDTH)
    bu0 = expand(0)
    bu1 = expand(1)

    t_glob = tile * tt + lax.broadcasted_iota(jnp.int32, (1, tt, POOL_GROUP), 1)
    d_blocks = []
    for gi, win in enumerate(POOL_WINDOWS):
        c0 = gi * POOL_GROUP
        s = halo_ref[:, :, c0:c0 + POOL_GROUP]
        cur = s[:, POOL_HALO:, :]
        k = 1
        while k < win:
            s = s + pltpu.roll(s, k, axis=1)
            k *= 2
        inv_cnt = 1.0 / jnp.minimum(t_glob + 1, win).astype(_F32)
        d = s[:, POOL_HALO:, :] * inv_cnt - cur
        d_blocks.append(d.reshape(rows, POOL_GROUP).astype(_BF16))
    halo_ref[:, 0:POOL_HALO, :] = halo_ref[:, tt:tt + POOL_HALO, :]

    h_pool = proj(_OFF_POOL_GATE, POOL_WIDTH)
    sp0 = scan(0, bu0)
    bu2 = expand(2)
    bu3 = expand(3)
    pooled = [_dot(d_blocks[gi], pool_w_ref[gi])
              * pool_scale_ref[:, gi * POOL_GROUP:(gi + 1) * POOL_GROUP]
              for gi in range(len(POOL_WINDOWS))]
    t_gpool = jnp.tanh(proj(_OFF_G_POOL, D_MODEL))
    sp1 = scan(1, bu1)
    y_fold = [contract(0, sp0)]
    y_pool = (jnp.concatenate(pooled, axis=1) * _half_silu(h_pool)).astype(_BF16)
    y_pool = y_pool + jnp.concatenate([order_zero] * (POOL_WIDTH // LANES), axis=1)
    half_bp = _dot(y_pool, wbp_ref[...])
    sp2 = scan(2, bu2)
    h_ssm = proj(_OFF_SSM_GATE, SSM_WIDTH)
    y_fold.append(contract(1, sp1))
    merged_pool = half_bp + t_gpool * half_bp
    sp3 = scan(3, bu3)
    t_gssm = jnp.tanh(proj(_OFF_G_SSM, D_MODEL))
    y_fold.append(contract(2, sp2))
    y_fold.append(contract(3, sp3))

    for c in range(SSM_LANE_TILES):
        for t in range(tt):
            k, e = divmod(t, TIME_FOLD)
            yt_ref[c, pl.ds(t, batch, stride=pitch), :] = (
                y_fold[c][k * batch:(k + 1) * batch, e * LANES:(e + 1) * LANES])
    y_cols = []
    for c in range(SSM_LANE_TILES):
        y_cols.append(jnp.concatenate(
            [yt_ref[c, b * pitch:b * pitch + tt, :] for b in range(batch)], axis=0))
    y = jnp.concatenate(y_cols, axis=1) + dskip_ref[...] * u
    half_y = 0.5 * y
    g = half_y + half_y * jnp.tanh(y * (_GELU_C + (_GELU_C * 0.044715) * (y * y)))
    half_gate = D_MODEL // 2
    t_ple_lo = jnp.tanh(proj(_OFF_PLE_GATE, half_gate))
    half_q = _dot(g.astype(_BF16), glu_w_ref[...]) + glu_b_ref[...]
    t_ple_hi = jnp.tanh(proj(_OFF_PLE_GATE + half_gate, half_gate))
    t_ple = jnp.concatenate([t_ple_lo, t_ple_hi], axis=1)
    half_g = 0.5 * g
    y_ssm = ((half_g + half_g * jnp.tanh(half_q)) * _half_silu(h_ssm)).astype(_BF16)
    half_bs = _dot(y_ssm, wbs_ref[...])
    pb = p_ref[...].reshape(rows, PLE_DIM).astype(_BF16)
    half_ple = _dot(pb, w_ple_ref[...])
    merged_ref[...] = (merged_pool + half_bs + t_gssm * half_bs).astype(_BF16)
    ple = half_ple + t_ple * half_ple
    base_ref[...] = alpha * x_ref[...].reshape(rows, D_MODEL) + ple


def _cmul(ar, ai, br, bi):
    return ar * br - ai * bi, ar * bi + ai * br


def _spread_lanes(x2d, period):
    onehot = (jnp.arange(LANES)[None, :] % period == jnp.arange(period)[:, None]).astype(_F32)
    return jnp.dot(x2d, onehot, precision=_EXACT)


def _ssm_operands(a_re, a_im, log_dt, b_re, b_im, c_re, c_im):
    f32 = _F32
    dt = jnp.exp(log_dt.astype(f32))[:, None]
    lr = a_re.astype(f32)
    li = a_im.astype(f32)
    mag = jnp.exp(lr * dt)
    abar_r = mag * jnp.cos(li * dt)
    abar_i = mag * jnp.sin(li * dt)
    den = lr * lr + li * li
    zr, zi = _cmul(abar_r - 1.0, abar_i, lr, -li)
    zr = zr / den
    zi = zi / den
    bbar_r, bbar_i = _cmul(zr[..., None], zi[..., None], b_re.astype(f32), b_im.astype(f32))

    a2_r, a2_i = _cmul(abar_r, abar_i, abar_r, abar_i)
    ab_r, ab_i = _cmul(abar_r[..., None], abar_i[..., None], bbar_r, bbar_i)
    cr = c_re.astype(f32)
    ci = c_im.astype(f32)
    ca_r, ca_i = _cmul(cr, ci, abar_r[:, None, :], abar_i[:, None, :])
    ca2_r, ca2_i = _cmul(cr, ci, a2_r[:, None, :], a2_i[:, None, :])

    def re_prod(xr, xi, yr, yi):
        return (jnp.einsum("ghp,gpk->ghk", xr, yr, precision=_EXACT)
                - jnp.einsum("ghp,gpk->ghk", xi, yi, precision=_EXACT))

    cb = re_prod(cr, ci, bbar_r, bbar_i)
    cab = re_prod(ca_r, ca_i, bbar_r, bbar_i)

    ar2 = a2_r.reshape(N_PAIRS, LANES)
    ai2 = a2_i.reshape(N_PAIRS, LANES)

    n_state_rows = N_PAIRS * PAIR_COLS
    row = jnp.arange(n_state_rows)[:, None]
    lane_group = jnp.arange(LANES)[None, :] // SSM_GROUP_CH
    src_group = ((row // PAIR_COLS) % PAIRS_PER_LANE_TILE) * GROUPS_PER_PAIR + (row // SSM_STATE) % 2
    keep_b = lane_group == src_group

    def state_rows(xr, xi):
        shape = (N_PAIRS, GROUPS_PER_PAIR * SSM_STATE, SSM_GROUP_CH)
        x = jnp.stack([xr.reshape(shape), xi.reshape(shape)], axis=1).reshape(
            n_state_rows, SSM_GROUP_CH)
        return jnp.where(keep_b, _spread_lanes(x, SSM_GROUP_CH), 0.0)

    wbt = jnp.concatenate([state_rows(ab_r, ab_i), state_rows(bbar_r, bbar_i)], axis=1)
    wbt = wbt.reshape(N_PAIRS, PAIR_COLS, FOLD_COLS)

    ch = jnp.arange(SSM_WIDTH)[:, None]
    kcol = jnp.arange(BLOCK_COLS)[None, :]
    dst_group = (ch // SSM_GROUP_CH) % GROUPS_PER_LANE_TILE
    col_group = (kcol // PAIR_COLS) * GROUPS_PER_PAIR + (kcol // SSM_STATE) % 2
    keep_c = dst_group == col_group

    def out_rows(xr, xi):
        x2r = xr.reshape(SSM_WIDTH, SSM_STATE)
        x2i = -xi.reshape(SSM_WIDTH, SSM_STATE)
        pair = jnp.concatenate([x2r, x2r, x2i, x2i], axis=1)
        full = jnp.concatenate([pair] * PAIRS_PER_LANE_TILE, axis=1)
        return jnp.where(keep_c, full, 0.0).reshape(SSM_LANE_TILES, LANES, BLOCK_COLS)

    wct = jnp.concatenate([out_rows(ca_r, ca_i), out_rows(ca2_r, ca2_i)], axis=1)

    keep_d = lane_group == dst_group

    def direct(m):
        x = _spread_lanes(m.reshape(SSM_WIDTH, SSM_GROUP_CH), SSM_GROUP_CH)
        return jnp.where(keep_d, x, 0.0).reshape(SSM_LANE_TILES, LANES, LANES)

    d_cb = direct(cb)
    wdt = jnp.concatenate([
        jnp.concatenate([d_cb, jnp.zeros_like(d_cb)], axis=2),
        jnp.concatenate([direct(cab), d_cb], axis=2)], axis=1)
    return ar2, ai2, wbt.astype(_BF16), wct.astype(_BF16), wdt.astype(_BF16)


def _const_spec(shape):
    zeros = (0,) * len(shape)
    return pl.BlockSpec(shape, lambda i, _z=zeros: _z, pipeline_mode=pl.Buffered(1))


def _layer(x, p, w_in, pool_w, pool_scale, ssm, d_skip, glu_w, glu_b,
           w_branch_pool, w_branch_ssm, w_out, w_ple, ln_g, ln_b, alpha):
    batch, seq, d_model = x.shape
    assert d_model == D_MODEL and w_in.shape == (D_MODEL, IN_WIDTH)
    tt = TIME_TILE
    assert seq % tt == 0 and tt % SUBLANES == 0 and tt >= POOL_HALO and tt % TIME_FOLD == 0
    assert batch % (SUBLANES * ROW_SPLITS) == 0
    rows = batch * tt
    pitch = tt + ROW_PAD
    ar2, ai2, wbt, wct, wdt = ssm
    row = lambda v: v.astype(_F32).reshape(1, -1)
    col = jnp.arange(IN_WIDTH)
    is_value = (col < _OFF_POOL_GATE) | ((col >= _OFF_SSM_IN) & (col < _OFF_SSM_GATE))
    gate_scale = jnp.where(is_value, 1.0, 0.5).astype(_F32)
    operands = (
        x, p,
        (w_in * gate_scale).astype(_BF16), pool_w.astype(_BF16), row(pool_scale),
        wbt, wct, wdt, ar2, ai2, row(d_skip),
        (0.5 * glu_w).astype(_BF16), row(0.5 * glu_b),
        (0.5 * w_branch_pool).astype(_BF16), (0.5 * w_branch_ssm).astype(_BF16),
        w_out.astype(_BF16), (0.5 * w_ple).astype(_BF16), row(ln_g), row(ln_b),
    )
    n_tiles = seq // tt
    mixed_tile = lambda s: (0, jnp.minimum(s, n_tiles - 1), 0)
    finished_tile = lambda s: (0, jnp.maximum(s - 1, 0), 0)
    in_specs = [
        pl.BlockSpec((batch, tt, D_MODEL), mixed_tile),
        pl.BlockSpec((batch, tt, PLE_DIM), mixed_tile),
    ] + [_const_spec(op.shape) for op in operands[2:]]
    kernel = functools.partial(_layer_kernel, batch=batch, tt=tt, alpha=alpha)
    return pl.pallas_call(
        kernel,
        grid=(n_tiles + 1,),
        in_specs=in_specs,
        out_specs=pl.BlockSpec((batch, tt, D_MODEL), finished_tile),
        out_shape=jax.ShapeDtypeStruct(x.shape, x.dtype),
        scratch_shapes=[
            pltpu.VMEM((batch, POOL_HALO + tt, POOL_WIDTH), _F32),
            pltpu.VMEM((SSM_LANE_TILES, batch * pitch, LANES), _F32),
            pltpu.VMEM((rows // TIME_FOLD, SSM_LANE_TILES * FOLD_COLS), _BF16),
            pltpu.VMEM((batch, STATE_COLS), _F32),
            pltpu.VMEM((SSM_LANE_TILES, batch * pitch, LANES), _F32),
            pltpu.VMEM((rows, D_MODEL), _BF16),
            pltpu.VMEM((rows, D_MODEL), _F32),
        ],
        compiler_params=pltpu.CompilerParams(
            dimension_semantics=("arbitrary",),
            vmem_limit_bytes=VMEM_LIMIT_BYTES,
        ),
        name="hybrid_pool_s5_layer",
    )(*operands)


def kernel(x, p, w_in, pool_w, pool_scale, ssm_a_re, ssm_a_im, ssm_log_dt, ssm_b_re, ssm_b_im, ssm_c_re, ssm_c_im, ssm_d, glu_w, glu_b, w_branch_pool, w_branch_ssm, w_out, w_ple, ln_g, ln_b):
    depth = w_in.shape[0]
    alpha = (2.0 * depth) ** 0.25
    for i in range(depth):
        ssm = _ssm_operands(ssm_a_re[i], ssm_a_im[i], ssm_log_dt[i], ssm_b_re[i],
                            ssm_b_im[i], ssm_c_re[i], ssm_c_im[i])
        x = _layer(x, p[i], w_in[i], pool_w[i], pool_scale[i], ssm, ssm_d[i], glu_w[i],
                   glu_b[i], w_branch_pool[i], w_branch_ssm[i], w_out[i], w_ple[i],
                   ln_g[i], ln_b[i], alpha)
    return x
```

```python
import functools
import math

import jax
import jax.numpy as jnp
import numpy as np
from jax import lax
from jax.experimental import pallas as pl
from jax.experimental.pallas import tpu as pltpu

D_MODEL = 1024
PLE_DIM = 256
POOL_WIDTH = D_MODEL
POOL_WINDOWS = (2, 4, 8, 16)
POOL_GROUP = POOL_WIDTH // len(POOL_WINDOWS)
POOL_HALO = max(POOL_WINDOWS)
SSM_WIDTH = D_MODEL // 2
SSM_GROUP_CH = 16
SSM_GROUPS = SSM_WIDTH // SSM_GROUP_CH
SSM_STATE = 64
LN_EPS = 1e-5

_OFF_POOL_IN = 0
_OFF_POOL_GATE = POOL_WIDTH
_OFF_SSM_IN = 2 * POOL_WIDTH
_OFF_SSM_GATE = 2 * POOL_WIDTH + SSM_WIDTH
_OFF_G_POOL = 2 * POOL_WIDTH + 2 * SSM_WIDTH
_OFF_G_SSM = _OFF_G_POOL + D_MODEL
_OFF_PLE_GATE = _OFF_G_SSM + D_MODEL
IN_WIDTH = _OFF_PLE_GATE + D_MODEL

LANES = 128
SUBLANES = 8
MXU_DIM = 256
VMEM_LIMIT_BYTES = 60 * 1024 * 1024

GROUPS_PER_PAIR = LANES // SSM_STATE
N_PAIRS = SSM_GROUPS // GROUPS_PER_PAIR
PAIR_COLS = 2 * LANES
GROUPS_PER_LANE_TILE = LANES // SSM_GROUP_CH
PAIRS_PER_LANE_TILE = GROUPS_PER_LANE_TILE // GROUPS_PER_PAIR
SSM_LANE_TILES = SSM_WIDTH // LANES
BLOCK_COLS = PAIRS_PER_LANE_TILE * PAIR_COLS
STATE_COLS = SSM_LANE_TILES * BLOCK_COLS
TIME_FOLD = 2
FOLD_COLS = TIME_FOLD * LANES

TIME_TILE = 32
TILES_PER_STEP = 1
ROW_PAD = 8
ROW_SPLITS = 2

_BF16 = jnp.bfloat16
_F32 = jnp.float32
_GELU_C = math.sqrt(2.0 / math.pi)
_CONTRACT_LAST = (((1,), (1,)), ((), ()))
_EXACT = lax.Precision.HIGHEST


def _dot(a, b):
    return jnp.dot(a, b, preferred_element_type=_F32)


def _dot_t(a, b_t):
    return lax.dot_general(a, b_t, _CONTRACT_LAST, preferred_element_type=_F32)


def _half_silu(h):
    return h + h * jnp.tanh(h)


def _layer_kernel(x_ref, p_ref, w_in_ref, pool_w_ref, pool_scale_ref, wbt_ref, wct_ref, wdt_ref,
                  ar_ref, ai_ref, dskip_ref, glu_w_ref, glu_b_ref, wbp_ref, wbs_ref,
                  w_out_ref, w_ple_ref, lng_ref, lnb_ref, o_ref,
                  halo_ref, ut_ref, utb_ref, zstate_ref, yt_ref, merged_ref, base_ref,
                  *, batch, tt, alpha):
    rows = batch * tt
    step = pl.program_id(0)
    last = pl.num_programs(0) - 1

    @pl.when(step == 0)
    def _():
        halo_ref[:, 0:POOL_HALO, :] = jnp.zeros((batch, POOL_HALO, POOL_WIDTH), _F32)
        zstate_ref[...] = jnp.zeros_like(zstate_ref)
        merged_ref[...] = jnp.zeros_like(merged_ref)
        base_ref[...] = jnp.zeros_like(base_ref)

    def finish_previous():
        h = base_ref[...] + _dot(merged_ref[...], w_out_ref[...])
        mu = jnp.mean(h, axis=-1, keepdims=True)
        hc = h - mu
        var = jnp.mean(hc * hc, axis=-1, keepdims=True)
        out = hc * lax.rsqrt(var + LN_EPS) * lng_ref[...] + lnb_ref[...]
        o_ref[...] = out.reshape(batch, tt, D_MODEL).astype(o_ref.dtype)
        return out

    def finish_previous_ordered():
        bits = lax.bitcast_convert_type(finish_previous(), jnp.int32)
        folded = bits[:, 0:LANES]
        for c in range(1, D_MODEL // LANES):
            folded = folded | bits[:, c * LANES:(c + 1) * LANES]
        sixteen = jnp.int32(16)
        zero = lax.shift_right_logical(lax.shift_right_logical(folded, sixteen), sixteen)
        return zero.astype(_F32).astype(_BF16)

    @pl.when(step < last)
    def _():
        _tile(x_ref, p_ref, w_in_ref, pool_w_ref, pool_scale_ref, wbt_ref, wct_ref, wdt_ref,
              ar_ref, ai_ref, dskip_ref, glu_w_ref, glu_b_ref, wbp_ref, wbs_ref,
              w_ple_ref, halo_ref, ut_ref, utb_ref, zstate_ref, yt_ref, merged_ref, base_ref,
              finish_previous_ordered, tile=step, batch=batch, tt=tt, alpha=alpha)

    @pl.when(step == last)
    def _():
        finish_previous()


def _tile(x_ref, p_ref, w_in_ref, pool_w_ref, pool_scale_ref, wbt_ref, wct_ref, wdt_ref,
          ar_ref, ai_ref, dskip_ref, glu_w_ref, glu_b_ref, wbp_ref, wbs_ref,
          w_ple_ref, halo_ref, ut_ref, utb_ref, zstate_ref, yt_ref, merged_ref, base_ref,
          finish_previous_ordered, *, tile, batch, tt, alpha):
    rows = batch * tt
    folds = tt // TIME_FOLD
    pitch = tt + ROW_PAD

    order_zero = finish_previous_ordered()
    xb = x_ref[...].reshape(rows, D_MODEL).astype(_BF16)

    def proj(off, width):
        return _dot(xb, w_in_ref[:, off:off + width])

    def fold_inputs(c):
        return utb_ref[:, c * FOLD_COLS:(c + 1) * FOLD_COLS]

    def expand(c):
        cols = []
        for q in range(PAIRS_PER_LANE_TILE):
            j = c * PAIRS_PER_LANE_TILE + q
            cols.append(_dot_t(fold_inputs(c), wbt_ref[j]))
        return jnp.concatenate(cols, axis=1)

    def scan(c, bu):
        l0 = c * BLOCK_COLS
        z = zstate_ref[:, l0:l0 + BLOCK_COLS]
        prev = []
        for k in range(folds):
            prev.append(z.astype(_BF16))
            buk = bu[k * batch:(k + 1) * batch, :]
            out = []
            for q in range(PAIRS_PER_LANE_TILE):
                j = c * PAIRS_PER_LANE_TILE + q
                a_r = ar_ref[j:j + 1, :]
                a_i = ai_ref[j:j + 1, :]
                o = q * PAIR_COLS
                z_r = z[:, o:o + LANES]
                z_i = z[:, o + LANES:o + PAIR_COLS]
                out.append(a_r * z_r - a_i * z_i + buk[:, o:o + LANES])
                out.append(a_r * z_i + a_i * z_r + buk[:, o + LANES:o + PAIR_COLS])
            z = jnp.concatenate(out, axis=1)
        zstate_ref[:, l0:l0 + BLOCK_COLS] = z
        return jnp.concatenate(prev, axis=0)

    def contract(c, s_prev):
        return _dot_t(s_prev, wct_ref[c]) + _dot_t(fold_inputs(c), wdt_ref[c])

    u = proj(_OFF_SSM_IN, SSM_WIDTH)
    pool_in = proj(_OFF_POOL_IN, POOL_WIDTH)

    for c in range(SSM_LANE_TILES):
        for b in range(batch):
            ut_ref[c, b * pitch:b * pitch + tt, :] = (
                u[b * tt:(b + 1) * tt, c * LANES:(c + 1) * LANES])
    for t in range(tt):
        k, e = divmod(t, TIME_FOLD)
        for c in range(SSM_LANE_TILES):
            v = ut_ref[c, pl.ds(t, batch, stride=pitch), :]
            l0 = c * FOLD_COLS + e * LANES
            utb_ref[k * batch:(k + 1) * batch, l0:l0 + LANES] = v.astype(_BF16)

    halo_ref[:, POOL_HALO:POOL_HALO + tt, :] = pool_in.reshape(batch, tt, POOL_WIDTH)
    bu0 = expand(0)
    bu1 = expand(1)

    t_glob = tile * tt + lax.broadcasted_iota(jnp.int32, (1, tt, POOL_GROUP), 1)
    d_blocks = []
    for gi, win in enumerate(POOL_WINDOWS):
        c0 = gi * POOL_GROUP
        s = halo_ref[:, :, c0:c0 + POOL_GROUP]
        cur = s[:, POOL_HALO:, :]
        k = 1
        while k < win:
            s = s + pltpu.roll(s, k, axis=1)
            k *= 2
        inv_cnt = 1.0 / jnp.minimum(t_glob + 1, win).astype(_F32)
        d = s[:, POOL_HALO:, :] * inv_cnt - cur
        d_blocks.append(d.reshape(rows, POOL_GROUP).astype(_BF16))
    halo_ref[:, 0:POOL_HALO, :] = halo_ref[:, tt:tt + POOL_HALO, :]

    h_pool = proj(_OFF_POOL_GATE, POOL_WIDTH)
    sp0 = scan(0, bu0)
    bu2 = expand(2)
    bu3 = expand(3)
    pooled = [_dot(d_blocks[gi], pool_w_ref[gi])
              * pool_scale_ref[:, gi * POOL_GROUP:(gi + 1) * POOL_GROUP]
              for gi in range(len(POOL_WINDOWS))]
    t_gpool = jnp.tanh(proj(_OFF_G_POOL, D_MODEL))
    sp1 = scan(1, bu1)
    y_fold = [contract(0, sp0)]
    y_pool = (jnp.concatenate(pooled, axis=1) * _half_silu(h_pool)).astype(_BF16)
    y_pool = y_pool + jnp.concatenate([order_zero] * (POOL_WIDTH // LANES), axis=1)
    half_bp = _dot(y_pool, wbp_ref[...])
    sp2 = scan(2, bu2)
    h_ssm = proj(_OFF_SSM_GATE, SSM_WIDTH)
    y_fold.append(contract(1, sp1))
    merged_pool = half_bp + t_gpool * half_bp
    sp3 = scan(3, bu3)
    t_gssm = jnp.tanh(proj(_OFF_G_SSM, D_MODEL))
    y_fold.append(contract(2, sp2))
    y_fold.append(contract(3, sp3))

    for c in range(SSM_LANE_TILES):
        for t in range(tt):
            k, e = divmod(t, TIME_FOLD)
            yt_ref[c, pl.ds(t, batch, stride=pitch), :] = (
                y_fold[c][k * batch:(k + 1) * batch, e * LANES:(e + 1) * LANES])
    y_cols = []
    for c in range(SSM_LANE_TILES):
        y_cols.append(jnp.concatenate(
            [yt_ref[c, b * pitch:b * pitch + tt, :] for b in range(batch)], axis=0))
    y = jnp.concatenate(y_cols, axis=1) + dskip_ref[...] * u
    half_y = 0.5 * y
    g = half_y + half_y * jnp.tanh(y * (_GELU_C + (_GELU_C * 0.044715) * (y * y)))
    half_gate = D_MODEL // 2
    t_ple_lo = jnp.tanh(proj(_OFF_PLE_GATE, half_gate))
    half_q = _dot(g.astype(_BF16), glu_w_ref[...]) + glu_b_ref[...]
    t_ple_hi = jnp.tanh(proj(_OFF_PLE_GATE + half_gate, half_gate))
    t_ple = jnp.concatenate([t_ple_lo, t_ple_hi], axis=1)
    half_g = 0.5 * g
    y_ssm = ((half_g + half_g * jnp.tanh(half_q)) * _half_silu(h_ssm)).astype(_BF16)
    half_bs = _dot(y_ssm, wbs_ref[...])
    pb = p_ref[...].reshape(rows, PLE_DIM).astype(_BF16)
    half_ple = _dot(pb, w_ple_ref[...])
    merged_ref[...] = (merged_pool + half_bs + t_gssm * half_bs).astype(_BF16)
    ple = half_ple + t_ple * half_ple
    base_ref[...] = alpha * x_ref[...].reshape(rows, D_MODEL) + ple


def _cmul(ar, ai, br, bi):
    return ar * br - ai * bi, ar * bi + ai * br


def _spread_lanes(x2d, period):
    onehot = (np.arange(LANES)[None, :] % period == np.arange(period)[:, None]).astype(np.float32)
    return jnp.dot(x2d, onehot, precision=_EXACT)


def _ssm_operands(a_re, a_im, log_dt, b_re, b_im, c_re, c_im):
    f32 = _F32
    dt = jnp.exp(log_dt.astype(f32))[:, None]
    lr = a_re.astype(f32)
    li = a_im.astype(f32)
    mag = jnp.exp(lr * dt)
    abar_r = mag * jnp.cos(li * dt)
    abar_i = mag * jnp.sin(li * dt)
    den = lr * lr + li * li
    zr, zi = _cmul(abar_r - 1.0, abar_i, lr, -li)
    zr = zr / den
    zi = zi / den
    bbar_r, bbar_i = _cmul(zr[..., None], zi[..., None], b_re.astype(f32), b_im.astype(f32))

    a2_r, a2_i = _cmul(abar_r, abar_i, abar_r, abar_i)
    ab_r, ab_i = _cmul(abar_r[..., None], abar_i[..., None], bbar_r, bbar_i)
    cr = c_re.astype(f32)
    ci = c_im.astype(f32)
    ca_r, ca_i = _cmul(cr, ci, abar_r[:, None, :], abar_i[:, None, :])
    ca2_r, ca2_i = _cmul(cr, ci, a2_r[:, None, :], a2_i[:, None, :])

    lhs = jnp.stack([jnp.concatenate([cr, -ci], axis=-1), jnp.concatenate([ca_r, -ca_i], axis=-1)])
    rhs = jnp.concatenate([bbar_r, bbar_i], axis=1)
    cb, cab = jnp.einsum("eghp,gpk->eghk", lhs, rhs, precision=_EXACT)

    ar2 = a2_r.reshape(N_PAIRS, LANES)
    ai2 = a2_i.reshape(N_PAIRS, LANES)

    n_state_rows = N_PAIRS * PAIR_COLS
    row = np.arange(n_state_rows)[:, None]
    lane_group = np.arange(LANES)[None, :] // SSM_GROUP_CH
    src_group = ((row // PAIR_COLS) % PAIRS_PER_LANE_TILE) * GROUPS_PER_PAIR + (row // SSM_STATE) % 2
    keep_b = lane_group == src_group

    def state_rows(xr, xi):
        shape = (N_PAIRS, GROUPS_PER_PAIR * SSM_STATE, SSM_GROUP_CH)
        return jnp.stack([xr.reshape(shape), xi.reshape(shape)], axis=1).reshape(
            n_state_rows, SSM_GROUP_CH)

    ch = np.arange(SSM_WIDTH)[:, None]
    kcol = np.arange(BLOCK_COLS)[None, :]
    dst_group = (ch // SSM_GROUP_CH) % GROUPS_PER_LANE_TILE
    col_group = (kcol // PAIR_COLS) * GROUPS_PER_PAIR + (kcol // SSM_STATE) % 2
    keep_c = dst_group == col_group

    def out_rows(xr, xi):
        x2r = xr.reshape(SSM_WIDTH, SSM_STATE)
        x2i = -xi.reshape(SSM_WIDTH, SSM_STATE)
        pair = jnp.concatenate([x2r, x2r, x2i, x2i], axis=1)
        full = jnp.concatenate([pair] * PAIRS_PER_LANE_TILE, axis=1)
        return jnp.where(keep_c, full, 0.0).reshape(SSM_LANE_TILES, LANES, BLOCK_COLS)

    wct = jnp.concatenate([out_rows(ca_r, ca_i), out_rows(ca2_r, ca2_i)], axis=1)

    keep_d = lane_group == dst_group

    narrow = jnp.concatenate([
        state_rows(ab_r, ab_i), state_rows(bbar_r, bbar_i),
        cb.reshape(SSM_WIDTH, SSM_GROUP_CH), cab.reshape(SSM_WIDTH, SSM_GROUP_CH)], axis=0)
    keep = np.concatenate([keep_b, keep_b, keep_d, keep_d], axis=0)
    wide = jnp.where(keep, _spread_lanes(narrow, SSM_GROUP_CH), 0.0).astype(_BF16)
    o_b, o_cb, o_cab = n_state_rows, 2 * n_state_rows, 2 * n_state_rows + SSM_WIDTH
    wbt = jnp.concatenate([wide[0:o_b], wide[o_b:o_cb]], axis=1).reshape(
        N_PAIRS, PAIR_COLS, FOLD_COLS)
    d_cb = wide[o_cb:o_cab].reshape(SSM_LANE_TILES, LANES, LANES)
    d_cab = wide[o_cab:].reshape(SSM_LANE_TILES, LANES, LANES)
    wdt = jnp.concatenate([
        jnp.concatenate([d_cb, jnp.zeros_like(d_cb)], axis=2),
        jnp.concatenate([d_cab, d_cb], axis=2)], axis=1)
    return ar2, ai2, wbt, wct.astype(_BF16), wdt


def _const_spec(shape):
    zeros = (0,) * len(shape)
    return pl.BlockSpec(shape, lambda i, _z=zeros: _z, pipeline_mode=pl.Buffered(1))


def _layer(x, p, w_in, pool_w, pool_scale, ssm, d_skip, glu_w, glu_b,
           w_branch_pool, w_branch_ssm, w_out, w_ple, ln_g, ln_b, alpha):
    batch, seq, d_model = x.shape
    assert d_model == D_MODEL and w_in.shape == (D_MODEL, IN_WIDTH)
    tt = TIME_TILE
    assert seq % tt == 0 and tt % SUBLANES == 0 and tt >= POOL_HALO and tt % TIME_FOLD == 0
    assert batch % (SUBLANES * ROW_SPLITS) == 0
    rows = batch * tt
    pitch = tt + ROW_PAD
    ar2, ai2, wbt, wct, wdt = ssm
    row = lambda v: v.astype(_F32).reshape(1, -1)
    col = np.arange(IN_WIDTH)
    is_value = (col < _OFF_POOL_GATE) | ((col >= _OFF_SSM_IN) & (col < _OFF_SSM_GATE))
    gate_scale = np.where(is_value, 1.0, 0.5).astype(np.float32)
    operands = (
        x, p,
        (w_in * gate_scale).astype(_BF16), pool_w.astype(_BF16), row(pool_scale),
        wbt, wct, wdt, ar2, ai2, row(d_skip),
        (0.5 * glu_w).astype(_BF16), row(0.5 * glu_b),
        (0.5 * w_branch_pool).astype(_BF16), (0.5 * w_branch_ssm).astype(_BF16),
        w_out.astype(_BF16), (0.5 * w_ple).astype(_BF16), row(ln_g), row(ln_b),
    )
    n_tiles = seq // tt
    mixed_tile = lambda s: (0, jnp.minimum(s, n_tiles - 1), 0)
    finished_tile = lambda s: (0, jnp.maximum(s - 1, 0), 0)
    in_specs = [
        pl.BlockSpec((batch, tt, D_MODEL), mixed_tile),
        pl.BlockSpec((batch, tt, PLE_DIM), mixed_tile),
    ] + [_const_spec(op.shape) for op in operands[2:]]
    kernel = functools.partial(_layer_kernel, batch=batch, tt=tt, alpha=alpha)
    return pl.pallas_call(
        kernel,
        grid=(n_tiles + 1,),
        in_specs=in_specs,
        out_specs=pl.BlockSpec((batch, tt, D_MODEL), finished_tile),
        out_shape=jax.ShapeDtypeStruct(x.shape, x.dtype),
        scratch_shapes=[
            pltpu.VMEM((batch, POOL_HALO + tt, POOL_WIDTH), _F32),
            pltpu.VMEM((SSM_LANE_TILES, batch * pitch, LANES), _F32),
            pltpu.VMEM((rows // TIME_FOLD, SSM_LANE_TILES * FOLD_COLS), _BF16),
            pltpu.VMEM((batch, STATE_COLS), _F32),
            pltpu.VMEM((SSM_LANE_TILES, batch * pitch, LANES), _F32),
            pltpu.VMEM((rows, D_MODEL), _BF16),
            pltpu.VMEM((rows, D_MODEL), _F32),
        ],
        compiler_params=pltpu.CompilerParams(
            dimension_semantics=("arbitrary",),
            vmem_limit_bytes=VMEM_LIMIT_BYTES,
        ),
        name="hybrid_pool_s5_layer",
    )(*operands)


def kernel(x, p, w_in, pool_w, pool_scale, ssm_a_re, ssm_a_im, ssm_log_dt, ssm_b_re, ssm_b_im, ssm_c_re, ssm_c_im, ssm_d, glu_w, glu_b, w_branch_pool, w_branch_ssm, w_out, w_ple, ln_g, ln_b):
    depth = w_in.shape[0]
    alpha = (2.0 * depth) ** 0.25
    for i in range(depth):
        ssm = _ssm_operands(ssm_a_re[i], ssm_a_im[i], ssm_log_dt[i], ssm_b_re[i],
                            ssm_b_im[i], ssm_c_re[i], ssm_c_im[i])
        x = _layer(x, p[i], w_in[i], pool_w[i], pool_scale[i], ssm, ssm_d[i], glu_w[i],
                   glu_b[i], w_branch_pool[i], w_branch_ssm[i], w_out[i], w_ple[i],
                   ln_g[i], ln_b[i], alpha)
    return x
```

```python
import functools
import math

import jax
import jax.numpy as jnp
from jax import lax
from jax.experimental import pallas as pl
from jax.experimental.pallas import tpu as pltpu

D_MODEL = 1024
PLE_DIM = 256
POOL_WIDTH = D_MODEL
POOL_WINDOWS = (2, 4, 8, 16)
POOL_GROUP = POOL_WIDTH // len(POOL_WINDOWS)
POOL_HALO = max(POOL_WINDOWS)
SSM_WIDTH = D_MODEL // 2
SSM_GROUP_CH = 16
SSM_GROUPS = SSM_WIDTH // SSM_GROUP_CH
SSM_STATE = 64
LN_EPS = 1e-5

_OFF_POOL_IN = 0
_OFF_POOL_GATE = POOL_WIDTH
_OFF_SSM_IN = 2 * POOL_WIDTH
_OFF_SSM_GATE = 2 * POOL_WIDTH + SSM_WIDTH
_OFF_G_POOL = 2 * POOL_WIDTH + 2 * SSM_WIDTH
_OFF_G_SSM = _OFF_G_POOL + D_MODEL
_OFF_PLE_GATE = _OFF_G_SSM + D_MODEL
IN_WIDTH = _OFF_PLE_GATE + D_MODEL

LANES = 128
SUBLANES = 8
MXU_DIM = 256
VMEM_LIMIT_BYTES = 60 * 1024 * 1024

GROUPS_PER_PAIR = LANES // SSM_STATE
N_PAIRS = SSM_GROUPS // GROUPS_PER_PAIR
PAIR_COLS = 2 * LANES
GROUPS_PER_LANE_TILE = LANES // SSM_GROUP_CH
PAIRS_PER_LANE_TILE = GROUPS_PER_LANE_TILE // GROUPS_PER_PAIR
SSM_LANE_TILES = SSM_WIDTH // LANES
BLOCK_COLS = PAIRS_PER_LANE_TILE * PAIR_COLS
STATE_COLS = SSM_LANE_TILES * BLOCK_COLS
TIME_FOLD = 2
FOLD_COLS = TIME_FOLD * LANES

TIME_TILE = 32
ROW_PAD = 8
W_IN_STAGE_ROWS = 64
WIDE_STAGE_ROWS = 128

_BF16 = jnp.bfloat16
_F32 = jnp.float32
_GELU_C = math.sqrt(2.0 / math.pi)
_CONTRACT_LAST = (((1,), (1,)), ((), ()))
_EXACT = lax.Precision.HIGHEST


def _dot(a, b):
    return jnp.dot(a, b, preferred_element_type=_F32)


def _dot_t(a, b_t):
    return lax.dot_general(a, b_t, _CONTRACT_LAST, preferred_element_type=_F32)


def _half_silu(h):
    return h + h * jnp.tanh(h)


def _stage_weights(jobs, stage_ref, sem):
    chunk = stage_ref.shape[1]
    chunks = [(src, dst, scale, r0)
              for src, dst, scale in jobs for r0 in range(0, src.shape[0], chunk)]

    def copy(i):
        src, _, _, r0 = chunks[i]
        return pltpu.make_async_copy(src.at[r0:r0 + chunk, :], stage_ref.at[i % 2], sem.at[i % 2])

    copy(0).start()
    for i, (_, dst, scale, r0) in enumerate(chunks):
        if i + 1 < len(chunks):
            copy(i + 1).start()
        copy(i).wait()
        dst[r0:r0 + chunk, :] = (stage_ref[i % 2] * scale).astype(_BF16)


def _layer_kernel(x_ref, p_ref, w_in_hbm, wbp_hbm, wbs_hbm, w_out_hbm, w_ple_hbm,
                  pool_w_ref, pool_scale_ref, wbt_ref, wct_ref, wdt_ref,
                  ar_ref, ai_ref, dskip_ref, glu_w_ref, glu_b_ref, lng_ref, lnb_ref, o_ref,
                  w_in_ref, wbp_ref, wbs_ref, w_out_ref, w_ple_ref,
                  in_stage_ref, wide_stage_ref, in_sem, wide_sem,
                  halo_ref, ut_ref, utb_ref, zstate_ref, yt_ref, merged_ref, base_ref,
                  *, batch, tt, alpha):
    rows = batch * tt
    step = pl.program_id(0)
    last = pl.num_programs(0) - 1

    @pl.when(step == 0)
    def _():
        halo_ref[:, 0:POOL_HALO, :] = jnp.zeros((batch, POOL_HALO, POOL_WIDTH), _F32)
        zstate_ref[...] = jnp.zeros_like(zstate_ref)
        merged_ref[...] = jnp.zeros_like(merged_ref)
        base_ref[...] = jnp.zeros_like(base_ref)
        col = lax.broadcasted_iota(jnp.int32, (1, IN_WIDTH), 1)
        is_value = (col < _OFF_POOL_GATE) | ((col >= _OFF_SSM_IN) & (col < _OFF_SSM_GATE))
        gate_scale = jnp.where(is_value, 1.0, 0.5).astype(_F32)
        _stage_weights([(w_in_hbm, w_in_ref, gate_scale)], in_stage_ref, in_sem)
        _stage_weights([(wbp_hbm, wbp_ref, 0.5), (w_out_hbm, w_out_ref, 1.0),
                        (wbs_hbm, wbs_ref, 0.5), (w_ple_hbm, w_ple_ref, 0.5)],
                       wide_stage_ref, wide_sem)

    def finish_previous():
        h = base_ref[...] + _dot(merged_ref[...], w_out_ref[...])
        mu = jnp.mean(h, axis=-1, keepdims=True)
        hc = h - mu
        var = jnp.mean(hc * hc, axis=-1, keepdims=True)
        out = hc * lax.rsqrt(var + LN_EPS) * lng_ref[...] + lnb_ref[...]
        o_ref[...] = out.reshape(batch, tt, D_MODEL).astype(o_ref.dtype)
        return out

    @pl.when(step < last)
    def _():
        out_prev = finish_previous()
        bits = lax.bitcast_convert_type(out_prev, jnp.int32)
        folded = bits[:, 0:LANES]
        for c in range(1, D_MODEL // LANES):
            folded = folded | bits[:, c * LANES:(c + 1) * LANES]
        sixteen = jnp.int32(16)
        zero = lax.shift_right_logical(lax.shift_right_logical(folded, sixteen), sixteen)
        _tile(x_ref, p_ref, w_in_ref, pool_w_ref, pool_scale_ref, wbt_ref, wct_ref, wdt_ref,
              ar_ref, ai_ref, dskip_ref, glu_w_ref, glu_b_ref, wbp_ref, wbs_ref,
              w_ple_ref, halo_ref, ut_ref, utb_ref, zstate_ref, yt_ref, merged_ref, base_ref,
              zero.astype(_F32).astype(_BF16),
              tile=step, batch=batch, tt=tt, alpha=alpha)

    @pl.when(step == last)
    def _():
        finish_previous()


def _tile(x_ref, p_ref, w_in_ref, pool_w_ref, pool_scale_ref, wbt_ref, wct_ref, wdt_ref,
          ar_ref, ai_ref, dskip_ref, glu_w_ref, glu_b_ref, wbp_ref, wbs_ref,
          w_ple_ref, halo_ref, ut_ref, utb_ref, zstate_ref, yt_ref, merged_ref, base_ref,
          order_zero, *, tile, batch, tt, alpha):
    rows = batch * tt
    folds = tt // TIME_FOLD
    pitch = tt + ROW_PAD

    xb = x_ref[...].reshape(rows, D_MODEL).astype(_BF16)

    def proj(off, width):
        return _dot(xb, w_in_ref[:, off:off + width])

    def fold_inputs(c):
        return utb_ref[:, c * FOLD_COLS:(c + 1) * FOLD_COLS]

    def expand(c):
        cols = []
        for q in range(PAIRS_PER_LANE_TILE):
            j = c * PAIRS_PER_LANE_TILE + q
            cols.append(_dot_t(fold_inputs(c), wbt_ref[j]))
        return jnp.concatenate(cols, axis=1)

    def scan(c, bu):
        l0 = c * BLOCK_COLS
        z = zstate_ref[:, l0:l0 + BLOCK_COLS]
        prev = []
        for k in range(folds):
            prev.append(z.astype(_BF16))
            buk = bu[k * batch:(k + 1) * batch, :]
            out = []
            for q in range(PAIRS_PER_LANE_TILE):
                j = c * PAIRS_PER_LANE_TILE + q
                a_r = ar_ref[j:j + 1, :]
                a_i = ai_ref[j:j + 1, :]
                o = q * PAIR_COLS
                z_r = z[:, o:o + LANES]
                z_i = z[:, o + LANES:o + PAIR_COLS]
                out.append(a_r * z_r - a_i * z_i + buk[:, o:o + LANES])
                out.append(a_r * z_i + a_i * z_r + buk[:, o + LANES:o + PAIR_COLS])
            z = jnp.concatenate(out, axis=1)
        zstate_ref[:, l0:l0 + BLOCK_COLS] = z
        return jnp.concatenate(prev, axis=0)

    def contract(c, s_prev):
        return _dot_t(s_prev, wct_ref[c]) + _dot_t(fold_inputs(c), wdt_ref[c])

    u = proj(_OFF_SSM_IN, SSM_WIDTH)
    pool_in = proj(_OFF_POOL_IN, POOL_WIDTH)

    for c in range(SSM_LANE_TILES):
        for b in range(batch):
            ut_ref[c, b * pitch:b * pitch + tt, :] = (
                u[b * tt:(b + 1) * tt, c * LANES:(c + 1) * LANES])
    for t in range(tt):
        k, e = divmod(t, TIME_FOLD)
        for c in range(SSM_LANE_TILES):
            v = ut_ref[c, pl.ds(t, batch, stride=pitch), :]
            l0 = c * FOLD_COLS + e * LANES
            utb_ref[k * batch:(k + 1) * batch, l0:l0 + LANES] = v.astype(_BF16)

    halo_ref[:, POOL_HALO:POOL_HALO + tt, :] = pool_in.reshape(batch, tt, POOL_WIDTH)
    bu0 = expand(0)
    bu1 = expand(1)

    t_glob = tile * tt + lax.broadcasted_iota(jnp.int32, (1, tt, POOL_GROUP), 1)
    d_blocks = []
    for gi, win in enumerate(POOL_WINDOWS):
        c0 = gi * POOL_GROUP
        s = halo_ref[:, :, c0:c0 + POOL_GROUP]
        cur = s[:, POOL_HALO:, :]
        k = 1
        while k < win:
            s = s + pltpu.roll(s, k, axis=1)
            k *= 2
        inv_cnt = 1.0 / jnp.minimum(t_glob + 1, win).astype(_F32)
        d = s[:, POOL_HALO:, :] * inv_cnt - cur
        d_blocks.append(d.reshape(rows, POOL_GROUP).astype(_BF16))
    halo_ref[:, 0:POOL_HALO, :] = halo_ref[:, tt:tt + POOL_HALO, :]

    h_pool = proj(_OFF_POOL_GATE, POOL_WIDTH)
    sp0 = scan(0, bu0)
    bu2 = expand(2)
    bu3 = expand(3)
    pooled = [_dot(d_blocks[gi], pool_w_ref[gi])
              * pool_scale_ref[:, gi * POOL_GROUP:(gi + 1) * POOL_GROUP]
              for gi in range(len(POOL_WINDOWS))]
    t_gpool = jnp.tanh(proj(_OFF_G_POOL, D_MODEL))
    sp1 = scan(1, bu1)
    y_fold = [contract(0, sp0)]
    y_pool = (jnp.concatenate(pooled, axis=1) * _half_silu(h_pool)).astype(_BF16)
    y_pool = y_pool + jnp.concatenate([order_zero] * (POOL_WIDTH // LANES), axis=1)
    half_bp = _dot(y_pool, wbp_ref[...])
    sp2 = scan(2, bu2)
    h_ssm = proj(_OFF_SSM_GATE, SSM_WIDTH)
    y_fold.append(contract(1, sp1))
    merged_pool = half_bp + t_gpool * half_bp
    sp3 = scan(3, bu3)
    t_gssm = jnp.tanh(proj(_OFF_G_SSM, D_MODEL))
    y_fold.append(contract(2, sp2))
    y_fold.append(contract(3, sp3))

    for c in range(SSM_LANE_TILES):
        for t in range(tt):
            k, e = divmod(t, TIME_FOLD)
            yt_ref[c, pl.ds(t, batch, stride=pitch), :] = (
                y_fold[c][k * batch:(k + 1) * batch, e * LANES:(e + 1) * LANES])
    y_cols = []
    for c in range(SSM_LANE_TILES):
        y_cols.append(jnp.concatenate(
            [yt_ref[c, b * pitch:b * pitch + tt, :] for b in range(batch)], axis=0))
    y = jnp.concatenate(y_cols, axis=1) + dskip_ref[...] * u
    half_y = 0.5 * y
    g = half_y + half_y * jnp.tanh(y * (_GELU_C + (_GELU_C * 0.044715) * (y * y)))
    half_gate = D_MODEL // 2
    t_ple_lo = jnp.tanh(proj(_OFF_PLE_GATE, half_gate))
    half_q = _dot(g.astype(_BF16), glu_w_ref[...]) + glu_b_ref[...]
    t_ple_hi = jnp.tanh(proj(_OFF_PLE_GATE + half_gate, half_gate))
    t_ple = jnp.concatenate([t_ple_lo, t_ple_hi], axis=1)
    half_g = 0.5 * g
    y_ssm = ((half_g + half_g * jnp.tanh(half_q)) * _half_silu(h_ssm)).astype(_BF16)
    half_bs = _dot(y_ssm, wbs_ref[...])
    pb = p_ref[...].reshape(rows, PLE_DIM).astype(_BF16)
    half_ple = _dot(pb, w_ple_ref[...])
    merged_ref[...] = (merged_pool + half_bs + t_gssm * half_bs).astype(_BF16)
    ple = half_ple + t_ple * half_ple
    base_ref[...] = alpha * x_ref[...].reshape(rows, D_MODEL) + ple


def _cmul(ar, ai, br, bi):
    return ar * br - ai * bi, ar * bi + ai * br


def _spread_lanes(x2d, period):
    onehot = (jnp.arange(LANES)[None, :] % period == jnp.arange(period)[:, None]).astype(_F32)
    return jnp.dot(x2d, onehot, precision=_EXACT)


def _ssm_operands(a_re, a_im, log_dt, b_re, b_im, c_re, c_im):
    f32 = _F32
    dt = jnp.exp(log_dt.astype(f32))[:, None]
    lr = a_re.astype(f32)
    li = a_im.astype(f32)
    mag = jnp.exp(lr * dt)
    abar_r = mag * jnp.cos(li * dt)
    abar_i = mag * jnp.sin(li * dt)
    den = lr * lr + li * li
    zr, zi = _cmul(abar_r - 1.0, abar_i, lr, -li)
    zr = zr / den
    zi = zi / den
    bbar_r, bbar_i = _cmul(zr[..., None], zi[..., None], b_re.astype(f32), b_im.astype(f32))

    a2_r, a2_i = _cmul(abar_r, abar_i, abar_r, abar_i)
    ab_r, ab_i = _cmul(abar_r[..., None], abar_i[..., None], bbar_r, bbar_i)
    cr = c_re.astype(f32)
    ci = c_im.astype(f32)
    ca_r, ca_i = _cmul(cr, ci, abar_r[:, None, :], abar_i[:, None, :])
    ca2_r, ca2_i = _cmul(cr, ci, a2_r[:, None, :], a2_i[:, None, :])

    def re_prod(xr, xi, yr, yi):
        return (jnp.einsum("ghp,gpk->ghk", xr, yr, precision=_EXACT)
                - jnp.einsum("ghp,gpk->ghk", xi, yi, precision=_EXACT))

    cb = re_prod(cr, ci, bbar_r, bbar_i)
    cab = re_prod(ca_r, ca_i, bbar_r, bbar_i)

    ar2 = a2_r.reshape(N_PAIRS, LANES)
    ai2 = a2_i.reshape(N_PAIRS, LANES)

    n_state_rows = N_PAIRS * PAIR_COLS
    row = jnp.arange(n_state_rows)[:, None]
    lane_group = jnp.arange(LANES)[None, :] // SSM_GROUP_CH
    src_group = ((row // PAIR_COLS) % PAIRS_PER_LANE_TILE) * GROUPS_PER_PAIR + (row // SSM_STATE) % 2
    keep_b = lane_group == src_group

    def state_rows(xr, xi):
        shape = (N_PAIRS, GROUPS_PER_PAIR * SSM_STATE, SSM_GROUP_CH)
        x = jnp.stack([xr.reshape(shape), xi.reshape(shape)], axis=1).reshape(
            n_state_rows, SSM_GROUP_CH)
        return jnp.where(keep_b, _spread_lanes(x, SSM_GROUP_CH), 0.0)

    wbt = jnp.concatenate([state_rows(ab_r, ab_i), state_rows(bbar_r, bbar_i)], axis=1)
    wbt = wbt.reshape(N_PAIRS, PAIR_COLS, FOLD_COLS)

    ch = jnp.arange(SSM_WIDTH)[:, None]
    kcol = jnp.arange(BLOCK_COLS)[None, :]
    dst_group = (ch // SSM_GROUP_CH) % GROUPS_PER_LANE_TILE
    col_group = (kcol // PAIR_COLS) * GROUPS_PER_PAIR + (kcol // SSM_STATE) % 2
    keep_c = dst_group == col_group

    def out_rows(xr, xi):
        x2r = xr.reshape(SSM_WIDTH, SSM_STATE)
        x2i = -xi.reshape(SSM_WIDTH, SSM_STATE)
        pair = jnp.concatenate([x2r, x2r, x2i, x2i], axis=1)
        full = jnp.concatenate([pair] * PAIRS_PER_LANE_TILE, axis=1)
        return jnp.where(keep_c, full, 0.0).reshape(SSM_LANE_TILES, LANES, BLOCK_COLS)

    wct = jnp.concatenate([out_rows(ca_r, ca_i), out_rows(ca2_r, ca2_i)], axis=1)

    keep_d = lane_group == dst_group

    def direct(m):
        x = _spread_lanes(m.reshape(SSM_WIDTH, SSM_GROUP_CH), SSM_GROUP_CH)
        return jnp.where(keep_d, x, 0.0).reshape(SSM_LANE_TILES, LANES, LANES)

    d_cb = direct(cb)
    wdt = jnp.concatenate([
        jnp.concatenate([d_cb, jnp.zeros_like(d_cb)], axis=2),
        jnp.concatenate([direct(cab), d_cb], axis=2)], axis=1)
    return ar2, ai2, wbt.astype(_BF16), wct.astype(_BF16), wdt.astype(_BF16)


def _const_spec(shape):
    zeros = (0,) * len(shape)
    return pl.BlockSpec(shape, lambda i, _z=zeros: _z, pipeline_mode=pl.Buffered(1))


def _layer(x, p, w_in, pool_w, pool_scale, ssm, d_skip, glu_w, glu_b,
           w_branch_pool, w_branch_ssm, w_out, w_ple, ln_g, ln_b, alpha):
    batch, seq, d_model = x.shape
    assert d_model == D_MODEL and w_in.shape == (D_MODEL, IN_WIDTH)
    tt = TIME_TILE
    assert seq % tt == 0 and tt % SUBLANES == 0 and tt >= POOL_HALO and tt % TIME_FOLD == 0
    assert batch % SUBLANES == 0
    rows = batch * tt
    pitch = tt + ROW_PAD
    ar2, ai2, wbt, wct, wdt = ssm
    row = lambda v: v.astype(_F32).reshape(1, -1)
    staged = (w_in, w_branch_pool, w_branch_ssm, w_out, w_ple)
    assert all(w.dtype == _F32 for w in staged)
    assert w_in.shape[0] % W_IN_STAGE_ROWS == 0
    assert all(w.shape[0] % WIDE_STAGE_ROWS == 0 and w.shape[1] == D_MODEL for w in staged[1:])
    resident = (
        pool_w.astype(_BF16), row(pool_scale), wbt, wct, wdt, ar2, ai2, row(d_skip),
        (0.5 * glu_w).astype(_BF16), row(0.5 * glu_b), row(ln_g), row(ln_b),
    )
    n_tiles = seq // tt
    mixed_tile = lambda s: (0, jnp.minimum(s, n_tiles - 1), 0)
    finished_tile = lambda s: (0, jnp.maximum(s - 1, 0), 0)
    in_specs = [
        pl.BlockSpec((batch, tt, D_MODEL), mixed_tile),
        pl.BlockSpec((batch, tt, PLE_DIM), mixed_tile),
    ] + [pl.BlockSpec(memory_space=pl.ANY) for _ in staged] + [
        _const_spec(op.shape) for op in resident]
    kernel = functools.partial(_layer_kernel, batch=batch, tt=tt, alpha=alpha)
    return pl.pallas_call(
        kernel,
        grid=(n_tiles + 1,),
        in_specs=in_specs,
        out_specs=pl.BlockSpec((batch, tt, D_MODEL), finished_tile),
        out_shape=jax.ShapeDtypeStruct(x.shape, x.dtype),
        scratch_shapes=[pltpu.VMEM(w.shape, _BF16) for w in staged] + [
            pltpu.VMEM((2, W_IN_STAGE_ROWS, IN_WIDTH), _F32),
            pltpu.VMEM((2, WIDE_STAGE_ROWS, D_MODEL), _F32),
            pltpu.SemaphoreType.DMA((2,)),
            pltpu.SemaphoreType.DMA((2,)),
            pltpu.VMEM((batch, POOL_HALO + tt, POOL_WIDTH), _F32),
            pltpu.VMEM((SSM_LANE_TILES, batch * pitch, LANES), _F32),
            pltpu.VMEM((rows // TIME_FOLD, SSM_LANE_TILES * FOLD_COLS), _BF16),
            pltpu.VMEM((batch, STATE_COLS), _F32),
            pltpu.VMEM((SSM_LANE_TILES, batch * pitch, LANES), _F32),
            pltpu.VMEM((rows, D_MODEL), _BF16),
            pltpu.VMEM((rows, D_MODEL), _F32),
        ],
        compiler_params=pltpu.CompilerParams(
            dimension_semantics=("arbitrary",),
            vmem_limit_bytes=VMEM_LIMIT_BYTES,
        ),
        name="hybrid_pool_s5_layer",
    )(x, p, *staged, *resident)


def kernel(x, p, w_in, pool_w, pool_scale, ssm_a_re, ssm_a_im, ssm_log_dt, ssm_b_re, ssm_b_im, ssm_c_re, ssm_c_im, ssm_d, glu_w, glu_b, w_branch_pool, w_branch_ssm, w_out, w_ple, ln_g, ln_b):
    depth = w_in.shape[0]
    alpha = (2.0 * depth) ** 0.25
    for i in range(depth):
        ssm = _ssm_operands(ssm_a_re[i], ssm_a_im[i], ssm_log_dt[i], ssm_b_re[i],
                            ssm_b_im[i], ssm_c_re[i], ssm_c_im[i])
        x = _layer(x, p[i], w_in[i], pool_w[i], pool_scale[i], ssm, ssm_d[i], glu_w[i],
                   glu_b[i], w_branch_pool[i], w_branch_ssm[i], w_out[i], w_ple[i],
                   ln_g[i], ln_b[i], alpha)
    return x
```

```python
import functools
import math

import jax
import jax.numpy as jnp
from jax import lax
from jax.experimental import pallas as pl
from jax.experimental.pallas import tpu as pltpu

D_MODEL = 1024
PLE_DIM = 256
POOL_WIDTH = D_MODEL
POOL_WINDOWS = (2, 4, 8, 16)
POOL_GROUP = POOL_WIDTH // len(POOL_WINDOWS)
POOL_HALO = max(POOL_WINDOWS)
SSM_WIDTH = D_MODEL // 2
SSM_GROUP_CH = 16
SSM_GROUPS = SSM_WIDTH // SSM_GROUP_CH
SSM_STATE = 64
LN_EPS = 1e-5

_OFF_POOL_IN = 0
_OFF_POOL_GATE = POOL_WIDTH
_OFF_SSM_IN = 2 * POOL_WIDTH
_OFF_SSM_GATE = 2 * POOL_WIDTH + SSM_WIDTH
_OFF_G_POOL = 2 * POOL_WIDTH + 2 * SSM_WIDTH
_OFF_G_SSM = _OFF_G_POOL + D_MODEL
_OFF_PLE_GATE = _OFF_G_SSM + D_MODEL
IN_WIDTH = _OFF_PLE_GATE + D_MODEL

LANES = 128
SUBLANES = 8
MXU_DIM = 256
VMEM_LIMIT_BYTES = 60 * 1024 * 1024

GROUPS_PER_PAIR = LANES // SSM_STATE
N_PAIRS = SSM_GROUPS // GROUPS_PER_PAIR
PAIR_COLS = 2 * LANES
GROUPS_PER_LANE_TILE = LANES // SSM_GROUP_CH
PAIRS_PER_LANE_TILE = GROUPS_PER_LANE_TILE // GROUPS_PER_PAIR
SSM_LANE_TILES = SSM_WIDTH // LANES
BLOCK_COLS = PAIRS_PER_LANE_TILE * PAIR_COLS
STATE_COLS = SSM_LANE_TILES * BLOCK_COLS
TIME_FOLD = 2
FOLD_COLS = TIME_FOLD * LANES

TIME_TILE = 32
ROW_PAD = 8
W_IN_STAGE_ROWS = 64
WIDE_STAGE_ROWS = 128
STAGE_DEPTH = 4

_BF16 = jnp.bfloat16
_F32 = jnp.float32
_GELU_C = math.sqrt(2.0 / math.pi)
_CONTRACT_LAST = (((1,), (1,)), ((), ()))
_EXACT = lax.Precision.HIGHEST


def _dot(a, b):
    return jnp.dot(a, b, preferred_element_type=_F32)


def _dot_t(a, b_t):
    return lax.dot_general(a, b_t, _CONTRACT_LAST, preferred_element_type=_F32)


def _half_silu(h):
    return h + h * jnp.tanh(h)


def _stage_weights(rings):
    plans = []
    for jobs, stage_ref, sem in rings:
        depth, chunk = stage_ref.shape[0], stage_ref.shape[1]
        chunks = [(src, dst, scale, r0)
                  for src, dst, scale in jobs for r0 in range(0, src.shape[0], chunk)]

        def copy(i, chunks=chunks, chunk=chunk, depth=depth, stage_ref=stage_ref, sem=sem):
            src, _, _, r0 = chunks[i]
            return pltpu.make_async_copy(
                src.at[r0:r0 + chunk, :], stage_ref.at[i % depth], sem.at[i % depth])

        plans.append((chunks, chunk, depth, stage_ref, copy))

    for chunks, _, depth, _, copy in plans:
        for i in range(min(depth, len(chunks))):
            copy(i).start()
    for chunks, chunk, depth, stage_ref, copy in plans:
        for i, (_, dst, scale, r0) in enumerate(chunks):
            copy(i).wait()
            dst[r0:r0 + chunk, :] = (stage_ref[i % depth] * scale).astype(_BF16)
            if i + depth < len(chunks):
                copy(i + depth).start()


def _layer_kernel(x_ref, p_ref, w_in_hbm, wbp_hbm, wbs_hbm, w_out_hbm, w_ple_hbm,
                  pool_w_ref, pool_scale_ref, wbt_ref, wct_ref, wdt_ref,
                  ar_ref, ai_ref, dskip_ref, glu_w_ref, glu_b_ref, lng_ref, lnb_ref, o_ref,
                  w_in_ref, wbp_ref, wbs_ref, w_out_ref, w_ple_ref,
                  in_stage_ref, wide_stage_ref, in_sem, wide_sem,
                  halo_ref, ut_ref, utb_ref, zstate_ref, yt_ref, merged_ref, base_ref,
                  *, batch, tt, alpha):
    rows = batch * tt
    step = pl.program_id(0)
    last = pl.num_programs(0) - 1

    @pl.when(step == 0)
    def _():
        halo_ref[:, 0:POOL_HALO, :] = jnp.zeros((batch, POOL_HALO, POOL_WIDTH), _F32)
        zstate_ref[...] = jnp.zeros_like(zstate_ref)
        merged_ref[...] = jnp.zeros_like(merged_ref)
        base_ref[...] = jnp.zeros_like(base_ref)
        col = lax.broadcasted_iota(jnp.int32, (1, IN_WIDTH), 1)
        is_value = (col < _OFF_POOL_GATE) | ((col >= _OFF_SSM_IN) & (col < _OFF_SSM_GATE))
        gate_scale = jnp.where(is_value, 1.0, 0.5).astype(_F32)
        _stage_weights([
            ([(w_in_hbm, w_in_ref, gate_scale)], in_stage_ref, in_sem),
            ([(wbp_hbm, wbp_ref, 0.5), (w_out_hbm, w_out_ref, 1.0),
              (wbs_hbm, wbs_ref, 0.5), (w_ple_hbm, w_ple_ref, 0.5)], wide_stage_ref, wide_sem)])

    def finish_previous():
        h = base_ref[...] + _dot(merged_ref[...], w_out_ref[...])
        mu = jnp.mean(h, axis=-1, keepdims=True)
        hc = h - mu
        var = jnp.mean(hc * hc, axis=-1, keepdims=True)
        out = hc * lax.rsqrt(var + LN_EPS) * lng_ref[...] + lnb_ref[...]
        o_ref[...] = out.reshape(batch, tt, D_MODEL).astype(o_ref.dtype)
        return out

    @pl.when(step < last)
    def _():
        out_prev = finish_previous()
        bits = lax.bitcast_convert_type(out_prev, jnp.int32)
        folded = bits[:, 0:LANES]
        for c in range(1, D_MODEL // LANES):
            folded = folded | bits[:, c * LANES:(c + 1) * LANES]
        sixteen = jnp.int32(16)
        zero = lax.shift_right_logical(lax.shift_right_logical(folded, sixteen), sixteen)
        _tile(x_ref, p_ref, w_in_ref, pool_w_ref, pool_scale_ref, wbt_ref, wct_ref, wdt_ref,
              ar_ref, ai_ref, dskip_ref, glu_w_ref, glu_b_ref, wbp_ref, wbs_ref,
              w_ple_ref, halo_ref, ut_ref, utb_ref, zstate_ref, yt_ref, merged_ref, base_ref,
              zero.astype(_F32).astype(_BF16),
              tile=step, batch=batch, tt=tt, alpha=alpha)

    @pl.when(step == last)
    def _():
        finish_previous()


def _tile(x_ref, p_ref, w_in_ref, pool_w_ref, pool_scale_ref, wbt_ref, wct_ref, wdt_ref,
          ar_ref, ai_ref, dskip_ref, glu_w_ref, glu_b_ref, wbp_ref, wbs_ref,
          w_ple_ref, halo_ref, ut_ref, utb_ref, zstate_ref, yt_ref, merged_ref, base_ref,
          order_zero, *, tile, batch, tt, alpha):
    rows = batch * tt
    folds = tt // TIME_FOLD
    pitch = tt + ROW_PAD

    xb = x_ref[...].reshape(rows, D_MODEL).astype(_BF16)

    def proj(off, width):
        return _dot(xb, w_in_ref[:, off:off + width])

    def fold_inputs(c):
        return utb_ref[:, c * FOLD_COLS:(c + 1) * FOLD_COLS]

    def expand(c):
        cols = []
        for q in range(PAIRS_PER_LANE_TILE):
            j = c * PAIRS_PER_LANE_TILE + q
            cols.append(_dot_t(fold_inputs(c), wbt_ref[j]))
        return jnp.concatenate(cols, axis=1)

    def scan(c, bu):
        l0 = c * BLOCK_COLS
        z = zstate_ref[:, l0:l0 + BLOCK_COLS]
        prev = []
        for k in range(folds):
            prev.append(z.astype(_BF16))
            buk = bu[k * batch:(k + 1) * batch, :]
            out = []
            for q in range(PAIRS_PER_LANE_TILE):
                j = c * PAIRS_PER_LANE_TILE + q
                a_r = ar_ref[j:j + 1, :]
                a_i = ai_ref[j:j + 1, :]
                o = q * PAIR_COLS
                z_r = z[:, o:o + LANES]
                z_i = z[:, o + LANES:o + PAIR_COLS]
                out.append(a_r * z_r - a_i * z_i + buk[:, o:o + LANES])
                out.append(a_r * z_i + a_i * z_r + buk[:, o + LANES:o + PAIR_COLS])
            z = jnp.concatenate(out, axis=1)
        zstate_ref[:, l0:l0 + BLOCK_COLS] = z
        return jnp.concatenate(prev, axis=0)

    def contract(c, s_prev):
        return _dot_t(s_prev, wct_ref[c]) + _dot_t(fold_inputs(c), wdt_ref[c])

    u = proj(_OFF_SSM_IN, SSM_WIDTH)
    pool_in = proj(_OFF_POOL_IN, POOL_WIDTH)

    for c in range(SSM_LANE_TILES):
        for b in range(batch):
            ut_ref[c, b * pitch:b * pitch + tt, :] = (
                u[b * tt:(b + 1) * tt, c * LANES:(c + 1) * LANES])
    for t in range(tt):
        k, e = divmod(t, TIME_FOLD)
        for c in range(SSM_LANE_TILES):
            v = ut_ref[c, pl.ds(t, batch, stride=pitch), :]
            l0 = c * FOLD_COLS + e * LANES
            utb_ref[k * batch:(k + 1) * batch, l0:l0 + LANES] = v.astype(_BF16)

    halo_ref[:, POOL_HALO:POOL_HALO + tt, :] = pool_in.reshape(batch, tt, POOL_WIDTH)
    bu0 = expand(0)
    bu1 = expand(1)

    t_glob = tile * tt + lax.broadcasted_iota(jnp.int32, (1, tt, POOL_GROUP), 1)
    d_blocks = []
    for gi, win in enumerate(POOL_WINDOWS):
        c0 = gi * POOL_GROUP
        s = halo_ref[:, :, c0:c0 + POOL_GROUP]
        cur = s[:, POOL_HALO:, :]
        k = 1
        while k < win:
            s = s + pltpu.roll(s, k, axis=1)
            k *= 2
        inv_cnt = 1.0 / jnp.minimum(t_glob + 1, win).astype(_F32)
        d = s[:, POOL_HALO:, :] * inv_cnt - cur
        d_blocks.append(d.reshape(rows, POOL_GROUP).astype(_BF16))
    halo_ref[:, 0:POOL_HALO, :] = halo_ref[:, tt:tt + POOL_HALO, :]

    h_pool = proj(_OFF_POOL_GATE, POOL_WIDTH)
    sp0 = scan(0, bu0)
    bu2 = expand(2)
    bu3 = expand(3)
    pooled = [_dot(d_blocks[gi], pool_w_ref[gi])
              * pool_scale_ref[:, gi * POOL_GROUP:(gi + 1) * POOL_GROUP]
              for gi in range(len(POOL_WINDOWS))]
    t_gpool = jnp.tanh(proj(_OFF_G_POOL, D_MODEL))
    sp1 = scan(1, bu1)
    y_fold = [contract(0, sp0)]
    y_pool = (jnp.concatenate(pooled, axis=1) * _half_silu(h_pool)).astype(_BF16)
    y_pool = y_pool + jnp.concatenate([order_zero] * (POOL_WIDTH // LANES), axis=1)
    half_bp = _dot(y_pool, wbp_ref[...])
    sp2 = scan(2, bu2)
    h_ssm = proj(_OFF_SSM_GATE, SSM_WIDTH)
    y_fold.append(contract(1, sp1))
    merged_pool = half_bp + t_gpool * half_bp
    sp3 = scan(3, bu3)
    t_gssm = jnp.tanh(proj(_OFF_G_SSM, D_MODEL))
    y_fold.append(contract(2, sp2))
    y_fold.append(contract(3, sp3))

    for c in range(SSM_LANE_TILES):
        for t in range(tt):
            k, e = divmod(t, TIME_FOLD)
            yt_ref[c, pl.ds(t, batch, stride=pitch), :] = (
                y_fold[c][k * batch:(k + 1) * batch, e * LANES:(e + 1) * LANES])
    y_cols = []
    for c in range(SSM_LANE_TILES):
        y_cols.append(jnp.concatenate(
            [yt_ref[c, b * pitch:b * pitch + tt, :] for b in range(batch)], axis=0))
    y = jnp.concatenate(y_cols, axis=1) + dskip_ref[...] * u
    half_y = 0.5 * y
    g = half_y + half_y * jnp.tanh(y * (_GELU_C + (_GELU_C * 0.044715) * (y * y)))
    half_gate = D_MODEL // 2
    t_ple_lo = jnp.tanh(proj(_OFF_PLE_GATE, half_gate))
    half_q = _dot(g.astype(_BF16), glu_w_ref[...]) + glu_b_ref[...]
    t_ple_hi = jnp.tanh(proj(_OFF_PLE_GATE + half_gate, half_gate))
    t_ple = jnp.concatenate([t_ple_lo, t_ple_hi], axis=1)
    half_g = 0.5 * g
    y_ssm = ((half_g + half_g * jnp.tanh(half_q)) * _half_silu(h_ssm)).astype(_BF16)
    half_bs = _dot(y_ssm, wbs_ref[...])
    pb = p_ref[...].reshape(rows, PLE_DIM).astype(_BF16)
    half_ple = _dot(pb, w_ple_ref[...])
    merged_ref[...] = (merged_pool + half_bs + t_gssm * half_bs).astype(_BF16)
    ple = half_ple + t_ple * half_ple
    base_ref[...] = alpha * x_ref[...].reshape(rows, D_MODEL) + ple


def _cmul(ar, ai, br, bi):
    return ar * br - ai * bi, ar * bi + ai * br


def _spread_lanes(x2d, period):
    onehot = (jnp.arange(LANES)[None, :] % period == jnp.arange(period)[:, None]).astype(_F32)
    return jnp.dot(x2d, onehot, precision=_EXACT)


def _ssm_operands(a_re, a_im, log_dt, b_re, b_im, c_re, c_im):
    f32 = _F32
    dt = jnp.exp(log_dt.astype(f32))[:, None]
    lr = a_re.astype(f32)
    li = a_im.astype(f32)
    mag = jnp.exp(lr * dt)
    abar_r = mag * jnp.cos(li * dt)
    abar_i = mag * jnp.sin(li * dt)
    den = lr * lr + li * li
    zr, zi = _cmul(abar_r - 1.0, abar_i, lr, -li)
    zr = zr / den
    zi = zi / den
    bbar_r, bbar_i = _cmul(zr[..., None], zi[..., None], b_re.astype(f32), b_im.astype(f32))

    a2_r, a2_i = _cmul(abar_r, abar_i, abar_r, abar_i)
    ab_r, ab_i = _cmul(abar_r[..., None], abar_i[..., None], bbar_r, bbar_i)
    cr = c_re.astype(f32)
    ci = c_im.astype(f32)
    ca_r, ca_i = _cmul(cr, ci, abar_r[:, None, :], abar_i[:, None, :])
    ca2_r, ca2_i = _cmul(cr, ci, a2_r[:, None, :], a2_i[:, None, :])

    def re_prod(xr, xi, yr, yi):
        return (jnp.einsum("ghp,gpk->ghk", xr, yr, precision=_EXACT)
                - jnp.einsum("ghp,gpk->ghk", xi, yi, precision=_EXACT))

    cb = re_prod(cr, ci, bbar_r, bbar_i)
    cab = re_prod(ca_r, ca_i, bbar_r, bbar_i)

    ar2 = a2_r.reshape(N_PAIRS, LANES)
    ai2 = a2_i.reshape(N_PAIRS, LANES)

    n_state_rows = N_PAIRS * PAIR_COLS
    row = jnp.arange(n_state_rows)[:, None]
    lane_group = jnp.arange(LANES)[None, :] // SSM_GROUP_CH
    src_group = ((row // PAIR_COLS) % PAIRS_PER_LANE_TILE) * GROUPS_PER_PAIR + (row // SSM_STATE) % 2
    keep_b = lane_group == src_group

    def state_rows(xr, xi):
        shape = (N_PAIRS, GROUPS_PER_PAIR * SSM_STATE, SSM_GROUP_CH)
        x = jnp.stack([xr.reshape(shape), xi.reshape(shape)], axis=1).reshape(
            n_state_rows, SSM_GROUP_CH)
        return jnp.where(keep_b, _spread_lanes(x, SSM_GROUP_CH), 0.0)

    wbt = jnp.concatenate([state_rows(ab_r, ab_i), state_rows(bbar_r, bbar_i)], axis=1)
    wbt = wbt.reshape(N_PAIRS, PAIR_COLS, FOLD_COLS)

    ch = jnp.arange(SSM_WIDTH)[:, None]
    kcol = jnp.arange(BLOCK_COLS)[None, :]
    dst_group = (ch // SSM_GROUP_CH) % GROUPS_PER_LANE_TILE
    col_group = (kcol // PAIR_COLS) * GROUPS_PER_PAIR + (kcol // SSM_STATE) % 2
    keep_c = dst_group == col_group

    def out_rows(xr, xi):
        x2r = xr.reshape(SSM_WIDTH, SSM_STATE)
        x2i = -xi.reshape(SSM_WIDTH, SSM_STATE)
        pair = jnp.concatenate([x2r, x2r, x2i, x2i], axis=1)
        full = jnp.concatenate([pair] * PAIRS_PER_LANE_TILE, axis=1)
        return jnp.where(keep_c, full, 0.0).reshape(SSM_LANE_TILES, LANES, BLOCK_COLS)

    wct = jnp.concatenate([out_rows(ca_r, ca_i), out_rows(ca2_r, ca2_i)], axis=1)

    keep_d = lane_group == dst_group

    def direct(m):
        x = _spread_lanes(m.reshape(SSM_WIDTH, SSM_GROUP_CH), SSM_GROUP_CH)
        return jnp.where(keep_d, x, 0.0).reshape(SSM_LANE_TILES, LANES, LANES)

    d_cb = direct(cb)
    wdt = jnp.concatenate([
        jnp.concatenate([d_cb, jnp.zeros_like(d_cb)], axis=2),
        jnp.concatenate([direct(cab), d_cb], axis=2)], axis=1)
    return ar2, ai2, wbt.astype(_BF16), wct.astype(_BF16), wdt.astype(_BF16)


def _const_spec(shape):
    zeros = (0,) * len(shape)
    return pl.BlockSpec(shape, lambda i, _z=zeros: _z, pipeline_mode=pl.Buffered(1))


def _layer(x, p, w_in, pool_w, pool_scale, ssm, d_skip, glu_w, glu_b,
           w_branch_pool, w_branch_ssm, w_out, w_ple, ln_g, ln_b, alpha):
    batch, seq, d_model = x.shape
    assert d_model == D_MODEL and w_in.shape == (D_MODEL, IN_WIDTH)
    tt = TIME_TILE
    assert seq % tt == 0 and tt % SUBLANES == 0 and tt >= POOL_HALO and tt % TIME_FOLD == 0
    assert batch % SUBLANES == 0
    rows = batch * tt
    pitch = tt + ROW_PAD
    ar2, ai2, wbt, wct, wdt = ssm
    row = lambda v: v.astype(_F32).reshape(1, -1)
    staged = (w_in, w_branch_pool, w_branch_ssm, w_out, w_ple)
    assert all(w.dtype == _F32 for w in staged)
    assert w_in.shape[0] % W_IN_STAGE_ROWS == 0
    assert all(w.shape[0] % WIDE_STAGE_ROWS == 0 and w.shape[1] == D_MODEL for w in staged[1:])
    resident = (
        pool_w.astype(_BF16), row(pool_scale), wbt, wct, wdt, ar2, ai2, row(d_skip),
        (0.5 * glu_w).astype(_BF16), row(0.5 * glu_b), row(ln_g), row(ln_b),
    )
    n_tiles = seq // tt
    mixed_tile = lambda s: (0, jnp.minimum(s, n_tiles - 1), 0)
    finished_tile = lambda s: (0, jnp.maximum(s - 1, 0), 0)
    in_specs = [
        pl.BlockSpec((batch, tt, D_MODEL), mixed_tile),
        pl.BlockSpec((batch, tt, PLE_DIM), mixed_tile),
    ] + [pl.BlockSpec(memory_space=pl.ANY) for _ in staged] + [
        _const_spec(op.shape) for op in resident]
    kernel = functools.partial(_layer_kernel, batch=batch, tt=tt, alpha=alpha)
    return pl.pallas_call(
        kernel,
        grid=(n_tiles + 1,),
        in_specs=in_specs,
        out_specs=pl.BlockSpec((batch, tt, D_MODEL), finished_tile),
        out_shape=jax.ShapeDtypeStruct(x.shape, x.dtype),
        scratch_shapes=[pltpu.VMEM(w.shape, _BF16) for w in staged] + [
            pltpu.VMEM((STAGE_DEPTH, W_IN_STAGE_ROWS, IN_WIDTH), _F32),
            pltpu.VMEM((STAGE_DEPTH, WIDE_STAGE_ROWS, D_MODEL), _F32),
            pltpu.SemaphoreType.DMA((STAGE_DEPTH,)),
            pltpu.SemaphoreType.DMA((STAGE_DEPTH,)),
            pltpu.VMEM((batch, POOL_HALO + tt, POOL_WIDTH), _F32),
            pltpu.VMEM((SSM_LANE_TILES, batch * pitch, LANES), _F32),
            pltpu.VMEM((rows // TIME_FOLD, SSM_LANE_TILES * FOLD_COLS), _BF16),
            pltpu.VMEM((batch, STATE_COLS), _F32),
            pltpu.VMEM((SSM_LANE_TILES, batch * pitch, LANES), _F32),
            pltpu.VMEM((rows, D_MODEL), _BF16),
            pltpu.VMEM((rows, D_MODEL), _F32),
        ],
        compiler_params=pltpu.CompilerParams(
            dimension_semantics=("arbitrary",),
            vmem_limit_bytes=VMEM_LIMIT_BYTES,
        ),
        name="hybrid_pool_s5_layer",
    )(x, p, *staged, *resident)


def kernel(x, p, w_in, pool_w, pool_scale, ssm_a_re, ssm_a_im, ssm_log_dt, ssm_b_re, ssm_b_im, ssm_c_re, ssm_c_im, ssm_d, glu_w, glu_b, w_branch_pool, w_branch_ssm, w_out, w_ple, ln_g, ln_b):
    depth = w_in.shape[0]
    alpha = (2.0 * depth) ** 0.25
    for i in range(depth):
        ssm = _ssm_operands(ssm_a_re[i], ssm_a_im[i], ssm_log_dt[i], ssm_b_re[i],
                            ssm_b_im[i], ssm_c_re[i], ssm_c_im[i])
        x = _layer(x, p[i], w_in[i], pool_w[i], pool_scale[i], ssm, ssm_d[i], glu_w[i],
                   glu_b[i], w_branch_pool[i], w_branch_ssm[i], w_out[i], w_ple[i],
                   ln_g[i], ln_b[i], alpha)
    return x
```

```python
import functools
import math

import jax
import jax.numpy as jnp
from jax import lax
from jax.experimental import pallas as pl
from jax.experimental.pallas import tpu as pltpu

D_MODEL = 1024
PLE_DIM = 256
POOL_WIDTH = D_MODEL
POOL_WINDOWS = (2, 4, 8, 16)
POOL_GROUP = POOL_WIDTH // len(POOL_WINDOWS)
POOL_HALO = max(POOL_WINDOWS)
SSM_WIDTH = D_MODEL // 2
SSM_GROUP_CH = 16
SSM_GROUPS = SSM_WIDTH // SSM_GROUP_CH
SSM_STATE = 64
LN_EPS = 1e-5

_OFF_POOL_IN = 0
_OFF_POOL_GATE = POOL_WIDTH
_OFF_SSM_IN = 2 * POOL_WIDTH
_OFF_SSM_GATE = 2 * POOL_WIDTH + SSM_WIDTH
_OFF_G_POOL = 2 * POOL_WIDTH + 2 * SSM_WIDTH
_OFF_G_SSM = _OFF_G_POOL + D_MODEL
_OFF_PLE_GATE = _OFF_G_SSM + D_MODEL
IN_WIDTH = _OFF_PLE_GATE + D_MODEL

LANES = 128
SUBLANES = 8
VMEM_LIMIT_BYTES = 60 * 1024 * 1024

GROUPS_PER_PAIR = LANES // SSM_STATE
N_PAIRS = SSM_GROUPS // GROUPS_PER_PAIR
PAIR_COLS = 2 * LANES
GROUPS_PER_LANE_TILE = LANES // SSM_GROUP_CH
PAIRS_PER_LANE_TILE = GROUPS_PER_LANE_TILE // GROUPS_PER_PAIR
SSM_LANE_TILES = SSM_WIDTH // LANES
BLOCK_COLS = PAIRS_PER_LANE_TILE * PAIR_COLS
STATE_COLS = SSM_LANE_TILES * BLOCK_COLS
TIME_FOLD = 2
FOLD_COLS = TIME_FOLD * LANES

TIME_TILE = 32
ROW_PAD = 8
W_IN_STAGE_ROWS = 64
WIDE_STAGE_ROWS = 128
STAGE_DEPTH = 4

_BF16 = jnp.bfloat16
_F32 = jnp.float32
_GELU_C = math.sqrt(2.0 / math.pi)
_CONTRACT_LAST = (((1,), (1,)), ((), ()))
_EXACT = lax.Precision.HIGHEST


def _dot(a, b):
    return jnp.dot(a, b, preferred_element_type=_F32)


def _dot_t(a, b_t):
    return lax.dot_general(a, b_t, _CONTRACT_LAST, preferred_element_type=_F32)


def _half_silu(h):
    return h + h * jnp.tanh(h)


def _stage_weights(rings):
    plans = []
    for jobs, stage_ref, sem in rings:
        depth, chunk = stage_ref.shape[0], stage_ref.shape[1]
        chunks = [(src, dst, scale, r0)
                  for src, dst, scale in jobs for r0 in range(0, src.shape[0], chunk)]

        def copy(i, chunks=chunks, chunk=chunk, depth=depth, stage_ref=stage_ref, sem=sem):
            src, _, _, r0 = chunks[i]
            return pltpu.make_async_copy(
                src.at[r0:r0 + chunk, :], stage_ref.at[i % depth], sem.at[i % depth])

        plans.append((chunks, chunk, depth, stage_ref, copy))

    for chunks, _, depth, _, copy in plans:
        for i in range(min(depth, len(chunks))):
            copy(i).start()
    order = sorted(((i + 1) / len(plan[0]), r, i)
                   for r, plan in enumerate(plans) for i in range(len(plan[0])))
    for _, r, i in order:
        chunks, chunk, depth, stage_ref, copy = plans[r]
        _, dst, scale, r0 = chunks[i]
        copy(i).wait()
        dst[r0:r0 + chunk, :] = (stage_ref[i % depth] * scale).astype(_BF16)
        if i + depth < len(chunks):
            copy(i + depth).start()


def _layer_kernel(x_ref, p_ref, w_in_hbm, wbp_hbm, wbs_hbm, w_out_hbm, w_ple_hbm,
                  pool_w_ref, pool_scale_ref, wbt_ref, wct_ref, wdt_ref,
                  ar_ref, ai_ref, dskip_ref, glu_w_ref, glu_b_ref, lng_ref, lnb_ref, o_ref,
                  w_in_ref, wbp_ref, wbs_ref, w_out_ref, w_ple_ref,
                  in_stage_ref, wide_stage_ref, in_sem, wide_sem,
                  halo_ref, ut_ref, utb_ref, zstate_ref, yt_ref, merged_ref, base_ref,
                  *, batch, tt, alpha):
    rows = batch * tt
    step = pl.program_id(0)
    last = pl.num_programs(0) - 1

    @pl.when(step == 0)
    def _():
        halo_ref[:, 0:POOL_HALO, :] = jnp.zeros((batch, POOL_HALO, POOL_WIDTH), _F32)
        zstate_ref[...] = jnp.zeros_like(zstate_ref)
        merged_ref[...] = jnp.zeros_like(merged_ref)
        base_ref[...] = jnp.zeros_like(base_ref)
        col = lax.broadcasted_iota(jnp.int32, (1, IN_WIDTH), 1)
        is_value = (col < _OFF_POOL_GATE) | ((col >= _OFF_SSM_IN) & (col < _OFF_SSM_GATE))
        gate_scale = jnp.where(is_value, 1.0, 0.5).astype(_F32)
        _stage_weights([
            ([(w_in_hbm, w_in_ref, gate_scale)], in_stage_ref, in_sem),
            ([(wbp_hbm, wbp_ref, 0.5), (w_out_hbm, w_out_ref, 1.0),
              (wbs_hbm, wbs_ref, 0.5), (w_ple_hbm, w_ple_ref, 0.5)], wide_stage_ref, wide_sem)])

    def finish_previous():
        h = base_ref[...] + _dot(merged_ref[...], w_out_ref[...])
        mu = jnp.mean(h, axis=-1, keepdims=True)
        hc = h - mu
        var = jnp.mean(hc * hc, axis=-1, keepdims=True)
        out = hc * lax.rsqrt(var + LN_EPS) * lng_ref[...] + lnb_ref[...]
        o_ref[...] = out.reshape(batch, tt, D_MODEL).astype(o_ref.dtype)
        return out

    @pl.when(step < last)
    def _():
        out_prev = finish_previous()
        bits = lax.bitcast_convert_type(out_prev, jnp.int32)
        folded = bits[:, 0:LANES]
        for c in range(1, D_MODEL // LANES):
            folded = folded | bits[:, c * LANES:(c + 1) * LANES]
        sixteen = jnp.int32(16)
        zero = lax.shift_right_logical(lax.shift_right_logical(folded, sixteen), sixteen)
        _tile(x_ref, p_ref, w_in_ref, pool_w_ref, pool_scale_ref, wbt_ref, wct_ref, wdt_ref,
              ar_ref, ai_ref, dskip_ref, glu_w_ref, glu_b_ref, wbp_ref, wbs_ref,
              w_ple_ref, halo_ref, ut_ref, utb_ref, zstate_ref, yt_ref, merged_ref, base_ref,
              zero.astype(_F32).astype(_BF16),
              tile=step, batch=batch, tt=tt, alpha=alpha)

    @pl.when(step == last)
    def _():
        finish_previous()


def _tile(x_ref, p_ref, w_in_ref, pool_w_ref, pool_scale_ref, wbt_ref, wct_ref, wdt_ref,
          ar_ref, ai_ref, dskip_ref, glu_w_ref, glu_b_ref, wbp_ref, wbs_ref,
          w_ple_ref, halo_ref, ut_ref, utb_ref, zstate_ref, yt_ref, merged_ref, base_ref,
          order_zero, *, tile, batch, tt, alpha):
    rows = batch * tt
    folds = tt // TIME_FOLD
    pitch = tt + ROW_PAD

    xb = x_ref[...].reshape(rows, D_MODEL).astype(_BF16)

    def proj(off, width):
        return _dot(xb, w_in_ref[:, off:off + width])

    def fold_inputs(c):
        return utb_ref[:, c * FOLD_COLS:(c + 1) * FOLD_COLS]

    def expand(c):
        cols = []
        for q in range(PAIRS_PER_LANE_TILE):
            j = c * PAIRS_PER_LANE_TILE + q
            cols.append(_dot_t(fold_inputs(c), wbt_ref[j]))
        return jnp.concatenate(cols, axis=1)

    def scan(c, bu):
        l0 = c * BLOCK_COLS
        z = zstate_ref[:, l0:l0 + BLOCK_COLS]
        prev = []
        for k in range(folds):
            prev.append(z.astype(_BF16))
            buk = bu[k * batch:(k + 1) * batch, :]
            out = []
            for q in range(PAIRS_PER_LANE_TILE):
                j = c * PAIRS_PER_LANE_TILE + q
                a_r = ar_ref[j:j + 1, :]
                a_i = ai_ref[j:j + 1, :]
                o = q * PAIR_COLS
                z_r = z[:, o:o + LANES]
                z_i = z[:, o + LANES:o + PAIR_COLS]
                out.append(a_r * z_r - a_i * z_i + buk[:, o:o + LANES])
                out.append(a_r * z_i + a_i * z_r + buk[:, o + LANES:o + PAIR_COLS])
            z = jnp.concatenate(out, axis=1)
        zstate_ref[:, l0:l0 + BLOCK_COLS] = z
        return jnp.concatenate(prev, axis=0)

    def contract(c, s_prev):
        return _dot_t(s_prev, wct_ref[c]) + _dot_t(fold_inputs(c), wdt_ref[c])

    u = proj(_OFF_SSM_IN, SSM_WIDTH)
    pool_in = proj(_OFF_POOL_IN, POOL_WIDTH)

    for c in range(SSM_LANE_TILES):
        for b in range(batch):
            ut_ref[c, b * pitch:b * pitch + tt, :] = (
                u[b * tt:(b + 1) * tt, c * LANES:(c + 1) * LANES])
    for t in range(tt):
        k, e = divmod(t, TIME_FOLD)
        for c in range(SSM_LANE_TILES):
            v = ut_ref[c, pl.ds(t, batch, stride=pitch), :]
            l0 = c * FOLD_COLS + e * LANES
            utb_ref[k * batch:(k + 1) * batch, l0:l0 + LANES] = v.astype(_BF16)

    halo_ref[:, POOL_HALO:POOL_HALO + tt, :] = pool_in.reshape(batch, tt, POOL_WIDTH)
    bu0 = expand(0)
    bu1 = expand(1)

    t_glob = tile * tt + lax.broadcasted_iota(jnp.int32, (1, tt, POOL_GROUP), 1)
    d_blocks = []
    for gi, win in enumerate(POOL_WINDOWS):
        c0 = gi * POOL_GROUP
        s = halo_ref[:, :, c0:c0 + POOL_GROUP]
        cur = s[:, POOL_HALO:, :]
        k = 1
        while k < win:
            s = s + pltpu.roll(s, k, axis=1)
            k *= 2
        inv_cnt = 1.0 / jnp.minimum(t_glob + 1, win).astype(_F32)
        d = s[:, POOL_HALO:, :] * inv_cnt - cur
        d_blocks.append(d.reshape(rows, POOL_GROUP).astype(_BF16))
    halo_ref[:, 0:POOL_HALO, :] = halo_ref[:, tt:tt + POOL_HALO, :]

    h_pool = proj(_OFF_POOL_GATE, POOL_WIDTH)
    sp0 = scan(0, bu0)
    bu2 = expand(2)
    bu3 = expand(3)
    t_gpool = jnp.tanh(proj(_OFF_G_POOL, D_MODEL))
    pooled = [_dot(d_blocks[gi], pool_w_ref[gi])
              * pool_scale_ref[:, gi * POOL_GROUP:(gi + 1) * POOL_GROUP]
              for gi in range(len(POOL_WINDOWS))]
    sp1 = scan(1, bu1)
    y_fold = [contract(0, sp0)]
    y_pool = (jnp.concatenate(pooled, axis=1) * _half_silu(h_pool)).astype(_BF16)
    y_pool = y_pool + jnp.concatenate([order_zero] * (POOL_WIDTH // LANES), axis=1)
    half_bp = _dot(y_pool, wbp_ref[...])
    sp2 = scan(2, bu2)
    h_ssm = proj(_OFF_SSM_GATE, SSM_WIDTH)
    y_fold.append(contract(1, sp1))
    merged_pool = half_bp + t_gpool * half_bp
    sp3 = scan(3, bu3)
    t_gssm = jnp.tanh(proj(_OFF_G_SSM, D_MODEL))
    y_fold.append(contract(2, sp2))
    y_fold.append(contract(3, sp3))

    for c in range(SSM_LANE_TILES):
        for t in range(tt):
            k, e = divmod(t, TIME_FOLD)
            yt_ref[c, pl.ds(t, batch, stride=pitch), :] = (
                y_fold[c][k * batch:(k + 1) * batch, e * LANES:(e + 1) * LANES])
    y_cols = []
    for c in range(SSM_LANE_TILES):
        y_cols.append(jnp.concatenate(
            [yt_ref[c, b * pitch:b * pitch + tt, :] for b in range(batch)], axis=0))
    y = jnp.concatenate(y_cols, axis=1) + dskip_ref[...] * u
    half_y = 0.5 * y
    g = half_y + half_y * jnp.tanh(y * (_GELU_C + (_GELU_C * 0.044715) * (y * y)))
    half_gate = D_MODEL // 2
    t_ple_lo = jnp.tanh(proj(_OFF_PLE_GATE, half_gate))
    half_q = _dot(g.astype(_BF16), glu_w_ref[...]) + glu_b_ref[...]
    t_ple_hi = jnp.tanh(proj(_OFF_PLE_GATE + half_gate, half_gate))
    t_ple = jnp.concatenate([t_ple_lo, t_ple_hi], axis=1)
    half_g = 0.5 * g
    y_ssm = ((half_g + half_g * jnp.tanh(half_q)) * _half_silu(h_ssm)).astype(_BF16)
    half_bs = _dot(y_ssm, wbs_ref[...])
    pb = p_ref[...].reshape(rows, PLE_DIM).astype(_BF16)
    half_ple = _dot(pb, w_ple_ref[...])
    merged_ref[...] = (merged_pool + half_bs + t_gssm * half_bs).astype(_BF16)
    ple = half_ple + t_ple * half_ple
    base_ref[...] = alpha * x_ref[...].reshape(rows, D_MODEL) + ple


def _cmul(ar, ai, br, bi):
    return ar * br - ai * bi, ar * bi + ai * br


def _spread_lanes(x2d, period):
    onehot = (jnp.arange(LANES)[None, :] % period == jnp.arange(period)[:, None]).astype(_F32)
    return jnp.dot(x2d, onehot, precision=_EXACT)


def _ssm_operands(a_re, a_im, log_dt, b_re, b_im, c_re, c_im):
    f32 = _F32
    dt = jnp.exp(log_dt.astype(f32))[:, None]
    lr = a_re.astype(f32)
    li = a_im.astype(f32)
    mag = jnp.exp(lr * dt)
    abar_r = mag * jnp.cos(li * dt)
    abar_i = mag * jnp.sin(li * dt)
    den = lr * lr + li * li
    zr, zi = _cmul(abar_r - 1.0, abar_i, lr, -li)
    zr = zr / den
    zi = zi / den
    bbar_r, bbar_i = _cmul(zr[..., None], zi[..., None], b_re.astype(f32), b_im.astype(f32))

    a2_r, a2_i = _cmul(abar_r, abar_i, abar_r, abar_i)
    ab_r, ab_i = _cmul(abar_r[..., None], abar_i[..., None], bbar_r, bbar_i)
    cr = c_re.astype(f32)
    ci = c_im.astype(f32)
    ca_r, ca_i = _cmul(cr, ci, abar_r[:, None, :], abar_i[:, None, :])
    ca2_r, ca2_i = _cmul(cr, ci, a2_r[:, None, :], a2_i[:, None, :])

    def re_prod(xr, xi, yr, yi):
        return (jnp.einsum("ghp,gpk->ghk", xr, yr, precision=_EXACT)
                - jnp.einsum("ghp,gpk->ghk", xi, yi, precision=_EXACT))

    cb = re_prod(cr, ci, bbar_r, bbar_i)
    cab = re_prod(ca_r, ca_i, bbar_r, bbar_i)

    ar2 = a2_r.reshape(N_PAIRS, LANES)
    ai2 = a2_i.reshape(N_PAIRS, LANES)

    n_state_rows = N_PAIRS * PAIR_COLS
    row = jnp.arange(n_state_rows)[:, None]
    lane_group = jnp.arange(LANES)[None, :] // SSM_GROUP_CH
    src_group = ((row // PAIR_COLS) % PAIRS_PER_LANE_TILE) * GROUPS_PER_PAIR + (row // SSM_STATE) % 2
    keep_b = lane_group == src_group

    def state_rows(xr, xi):
        shape = (N_PAIRS, GROUPS_PER_PAIR * SSM_STATE, SSM_GROUP_CH)
        x = jnp.stack([xr.reshape(shape), xi.reshape(shape)], axis=1).reshape(
            n_state_rows, SSM_GROUP_CH)
        return jnp.where(keep_b, _spread_lanes(x, SSM_GROUP_CH), 0.0)

    wbt = jnp.concatenate([state_rows(ab_r, ab_i), state_rows(bbar_r, bbar_i)], axis=1)
    wbt = wbt.reshape(N_PAIRS, PAIR_COLS, FOLD_COLS)

    ch = jnp.arange(SSM_WIDTH)[:, None]
    kcol = jnp.arange(BLOCK_COLS)[None, :]
    dst_group = (ch // SSM_GROUP_CH) % GROUPS_PER_LANE_TILE
    col_group = (kcol // PAIR_COLS) * GROUPS_PER_PAIR + (kcol // SSM_STATE) % 2
    keep_c = dst_group == col_group

    def out_rows(xr, xi):
        x2r = xr.reshape(SSM_WIDTH, SSM_STATE)
        x2i = -xi.reshape(SSM_WIDTH, SSM_STATE)
        pair = jnp.concatenate([x2r, x2r, x2i, x2i], axis=1)
        full = jnp.concatenate([pair] * PAIRS_PER_LANE_TILE, axis=1)
        return jnp.where(keep_c, full, 0.0).reshape(SSM_LANE_TILES, LANES, BLOCK_COLS)

    wct = jnp.concatenate([out_rows(ca_r, ca_i), out_rows(ca2_r, ca2_i)], axis=1)

    keep_d = lane_group == dst_group

    def direct(m):
        x = _spread_lanes(m.reshape(SSM_WIDTH, SSM_GROUP_CH), SSM_GROUP_CH)
        return jnp.where(keep_d, x, 0.0).reshape(SSM_LANE_TILES, LANES, LANES)

    d_cb = direct(cb)
    wdt = jnp.concatenate([
        jnp.concatenate([d_cb, jnp.zeros_like(d_cb)], axis=2),
        jnp.concatenate([direct(cab), d_cb], axis=2)], axis=1)
    return ar2, ai2, wbt.astype(_BF16), wct.astype(_BF16), wdt.astype(_BF16)


def _const_spec(shape):
    zeros = (0,) * len(shape)
    return pl.BlockSpec(shape, lambda i, _z=zeros: _z, pipeline_mode=pl.Buffered(1))


def _layer(x, p, w_in, pool_w, pool_scale, ssm, d_skip, glu_w, glu_b,
           w_branch_pool, w_branch_ssm, w_out, w_ple, ln_g, ln_b, alpha):
    batch, seq, d_model = x.shape
    assert d_model == D_MODEL and w_in.shape == (D_MODEL, IN_WIDTH)
    tt = TIME_TILE
    assert seq % tt == 0 and tt % SUBLANES == 0 and tt >= POOL_HALO and tt % TIME_FOLD == 0
    assert batch % SUBLANES == 0
    rows = batch * tt
    pitch = tt + ROW_PAD
    ar2, ai2, wbt, wct, wdt = ssm
    row = lambda v: v.astype(_F32).reshape(1, -1)
    staged = (w_in, w_branch_pool, w_branch_ssm, w_out, w_ple)
    assert all(w.dtype == _F32 for w in staged)
    assert w_in.shape[0] % W_IN_STAGE_ROWS == 0
    assert all(w.shape[0] % WIDE_STAGE_ROWS == 0 and w.shape[1] == D_MODEL for w in staged[1:])
    resident = (
        pool_w.astype(_BF16), row(pool_scale), wbt, wct, wdt, ar2, ai2, row(d_skip),
        (0.5 * glu_w).astype(_BF16), row(0.5 * glu_b), row(ln_g), row(ln_b),
    )
    n_tiles = seq // tt
    mixed_tile = lambda s: (0, jnp.minimum(s, n_tiles - 1), 0)
    finished_tile = lambda s: (0, jnp.maximum(s - 1, 0), 0)
    in_specs = [
        pl.BlockSpec((batch, tt, D_MODEL), mixed_tile),
        pl.BlockSpec((batch, tt, PLE_DIM), mixed_tile),
    ] + [pl.BlockSpec(memory_space=pl.ANY) for _ in staged] + [
        _const_spec(op.shape) for op in resident]
    kernel = functools.partial(_layer_kernel, batch=batch, tt=tt, alpha=alpha)
    return pl.pallas_call(
        kernel,
        grid=(n_tiles + 1,),
        in_specs=in_specs,
        out_specs=pl.BlockSpec((batch, tt, D_MODEL), finished_tile),
        out_shape=jax.ShapeDtypeStruct(x.shape, x.dtype),
        scratch_shapes=[pltpu.VMEM(w.shape, _BF16) for w in staged] + [
            pltpu.VMEM((STAGE_DEPTH, W_IN_STAGE_ROWS, IN_WIDTH), _F32),
            pltpu.VMEM((STAGE_DEPTH, WIDE_STAGE_ROWS, D_MODEL), _F32),
            pltpu.SemaphoreType.DMA((STAGE_DEPTH,)),
            pltpu.SemaphoreType.DMA((STAGE_DEPTH,)),
            pltpu.VMEM((batch, POOL_HALO + tt, POOL_WIDTH), _F32),
            pltpu.VMEM((SSM_LANE_TILES, batch * pitch, LANES), _F32),
            pltpu.VMEM((rows // TIME_FOLD, SSM_LANE_TILES * FOLD_COLS), _BF16),
            pltpu.VMEM((batch, STATE_COLS), _F32),
            pltpu.VMEM((SSM_LANE_TILES, batch * pitch, LANES), _F32),
            pltpu.VMEM((rows, D_MODEL), _BF16),
            pltpu.VMEM((rows, D_MODEL), _F32),
        ],
        compiler_params=pltpu.CompilerParams(
            dimension_semantics=("arbitrary",),
            vmem_limit_bytes=VMEM_LIMIT_BYTES,
        ),
        name="hybrid_pool_s5_layer",
    )(x, p, *staged, *resident)


def kernel(x, p, w_in, pool_w, pool_scale, ssm_a_re, ssm_a_im, ssm_log_dt, ssm_b_re, ssm_b_im, ssm_c_re, ssm_c_im, ssm_d, glu_w, glu_b, w_branch_pool, w_branch_ssm, w_out, w_ple, ln_g, ln_b):
    depth = w_in.shape[0]
    alpha = (2.0 * depth) ** 0.25
    for i in range(depth):
        ssm = _ssm_operands(ssm_a_re[i], ssm_a_im[i], ssm_log_dt[i], ssm_b_re[i],
                            ssm_b_im[i], ssm_c_re[i], ssm_c_im[i])
        x = _layer(x, p[i], w_in[i], pool_w[i], pool_scale[i], ssm, ssm_d[i], glu_w[i],
                   glu_b[i], w_branch_pool[i], w_branch_ssm[i], w_out[i], w_ple[i],
                   ln_g[i], ln_b[i], alpha)
    return x
```

```python
import functools
import math

import jax
import jax.numpy as jnp
from jax import lax
from jax.experimental import pallas as pl
from jax.experimental.pallas import tpu as pltpu

D_MODEL = 1024
PLE_DIM = 256
POOL_WIDTH = D_MODEL
POOL_WINDOWS = (2, 4, 8, 16)
POOL_GROUP = POOL_WIDTH // len(POOL_WINDOWS)
POOL_HALO = max(POOL_WINDOWS)
SSM_WIDTH = D_MODEL // 2
SSM_GROUP_CH = 16
SSM_GROUPS = SSM_WIDTH // SSM_GROUP_CH
SSM_STATE = 64
LN_EPS = 1e-5

_OFF_POOL_IN = 0
_OFF_POOL_GATE = POOL_WIDTH
_OFF_SSM_IN = 2 * POOL_WIDTH
_OFF_SSM_GATE = 2 * POOL_WIDTH + SSM_WIDTH
_OFF_G_POOL = 2 * POOL_WIDTH + 2 * SSM_WIDTH
_OFF_G_SSM = _OFF_G_POOL + D_MODEL
_OFF_PLE_GATE = _OFF_G_SSM + D_MODEL
IN_WIDTH = _OFF_PLE_GATE + D_MODEL

LANES = 128
SUBLANES = 8
VMEM_LIMIT_BYTES = 60 * 1024 * 1024

GROUPS_PER_PAIR = LANES // SSM_STATE
N_PAIRS = SSM_GROUPS // GROUPS_PER_PAIR
PAIR_COLS = 2 * LANES
GROUPS_PER_LANE_TILE = LANES // SSM_GROUP_CH
PAIRS_PER_LANE_TILE = GROUPS_PER_LANE_TILE // GROUPS_PER_PAIR
SSM_LANE_TILES = SSM_WIDTH // LANES
BLOCK_COLS = PAIRS_PER_LANE_TILE * PAIR_COLS
STATE_COLS = SSM_LANE_TILES * BLOCK_COLS
TIME_FOLD = 2
FOLD_COLS = TIME_FOLD * LANES

TIME_TILE = 32
ROW_PAD = 8
W_IN_STAGE_ROWS = 64
WIDE_STAGE_ROWS = 128
STAGE_DEPTH = 4

_BF16 = jnp.bfloat16
_F32 = jnp.float32
_GELU_C = math.sqrt(2.0 / math.pi)
_CONTRACT_LAST = (((1,), (1,)), ((), ()))
_EXACT = lax.Precision.HIGHEST


def _dot(a, b):
    return jnp.dot(a, b, preferred_element_type=_F32)


def _dot_t(a, b_t):
    return lax.dot_general(a, b_t, _CONTRACT_LAST, preferred_element_type=_F32)


def _half_silu(h):
    return h + h * jnp.tanh(h)


def _stage_weights(rings):
    plans = []
    for jobs, stage_ref, sem in rings:
        depth, chunk = stage_ref.shape[0], stage_ref.shape[1]
        chunks = [(src, dst, scale, r0)
                  for src, dst, scale in jobs for r0 in range(0, src.shape[0], chunk)]

        def copy(i, chunks=chunks, chunk=chunk, depth=depth, stage_ref=stage_ref, sem=sem):
            src, _, _, r0 = chunks[i]
            return pltpu.make_async_copy(
                src.at[r0:r0 + chunk, :], stage_ref.at[i % depth], sem.at[i % depth])

        plans.append((chunks, chunk, depth, stage_ref, copy))

    for chunks, _, depth, _, copy in plans:
        for i in range(min(depth, len(chunks))):
            copy(i).start()
    order = sorted(((i + 1) / len(plan[0]), r, i)
                   for r, plan in enumerate(plans) for i in range(len(plan[0])))
    for _, r, i in order:
        chunks, chunk, depth, stage_ref, copy = plans[r]
        _, dst, scale, r0 = chunks[i]
        copy(i).wait()
        dst[r0:r0 + chunk, :] = (stage_ref[i % depth] * scale).astype(_BF16)
        if i + depth < len(chunks):
            copy(i + depth).start()


def _layer_kernel(x_ref, p_ref, w_in_hbm, wbp_hbm, wbs_hbm, w_out_hbm, w_ple_hbm,
                  pool_w_ref, pool_scale_ref, wbt_ref, wct_ref, wdt_ref,
                  ar_ref, ai_ref, dskip_ref, glu_w_ref, glu_b_ref, lng_ref, lnb_ref, o_ref,
                  w_in_ref, wbp_ref, wbs_ref, w_out_ref, w_ple_ref,
                  in_stage_ref, wide_stage_ref, in_sem, wide_sem,
                  halo_ref, ut_ref, utb_ref, zstate_ref, yt_ref, merged_ref, base_ref,
                  *, batch, tt, alpha):
    rows = batch * tt
    step = pl.program_id(0)
    last = pl.num_programs(0) - 1

    @pl.when(step == 0)
    def _():
        halo_ref[:, 0:POOL_HALO, :] = jnp.zeros((batch, POOL_HALO, POOL_WIDTH), _F32)
        zstate_ref[...] = jnp.zeros_like(zstate_ref)
        merged_ref[...] = jnp.zeros_like(merged_ref)
        base_ref[...] = jnp.zeros_like(base_ref)
        col = lax.broadcasted_iota(jnp.int32, (1, IN_WIDTH), 1)
        is_value = (col < _OFF_POOL_GATE) | ((col >= _OFF_SSM_IN) & (col < _OFF_SSM_GATE))
        gate_scale = jnp.where(is_value, 1.0, 0.5).astype(_F32)
        _stage_weights([
            ([(w_in_hbm, w_in_ref, gate_scale)], in_stage_ref, in_sem),
            ([(wbp_hbm, wbp_ref, 0.5), (w_out_hbm, w_out_ref, 1.0),
              (wbs_hbm, wbs_ref, 0.5), (w_ple_hbm, w_ple_ref, 0.5)], wide_stage_ref, wide_sem)])
        for gi in range(len(POOL_WINDOWS)):
            c0 = _OFF_POOL_IN + gi * POOL_GROUP
            w_in_ref[:, c0:c0 + POOL_GROUP] = _dot(
                w_in_ref[:, c0:c0 + POOL_GROUP], pool_w_ref[gi]).astype(_BF16)

    def finish_previous():
        h = base_ref[...] + _dot(merged_ref[...], w_out_ref[...])
        mu = jnp.mean(h, axis=-1, keepdims=True)
        hc = h - mu
        var = jnp.mean(hc * hc, axis=-1, keepdims=True)
        out = hc * lax.rsqrt(var + LN_EPS) * lng_ref[...] + lnb_ref[...]
        o_ref[...] = out.reshape(batch, tt, D_MODEL).astype(o_ref.dtype)
        return out

    @pl.when(step < last)
    def _():
        out_prev = finish_previous()
        bits = lax.bitcast_convert_type(out_prev, jnp.int32)
        folded = bits[:, 0:LANES]
        for c in range(1, D_MODEL // LANES):
            folded = folded | bits[:, c * LANES:(c + 1) * LANES]
        sixteen = jnp.int32(16)
        zero = lax.shift_right_logical(lax.shift_right_logical(folded, sixteen), sixteen)
        _tile(x_ref, p_ref, w_in_ref, pool_scale_ref, wbt_ref, wct_ref, wdt_ref,
              ar_ref, ai_ref, dskip_ref, glu_w_ref, glu_b_ref, wbp_ref, wbs_ref,
              w_ple_ref, halo_ref, ut_ref, utb_ref, zstate_ref, yt_ref, merged_ref, base_ref,
              zero.astype(_F32).astype(_BF16),
              tile=step, batch=batch, tt=tt, alpha=alpha)

    @pl.when(step == last)
    def _():
        finish_previous()


def _tile(x_ref, p_ref, w_in_ref, pool_scale_ref, wbt_ref, wct_ref, wdt_ref,
          ar_ref, ai_ref, dskip_ref, glu_w_ref, glu_b_ref, wbp_ref, wbs_ref,
          w_ple_ref, halo_ref, ut_ref, utb_ref, zstate_ref, yt_ref, merged_ref, base_ref,
          order_zero, *, tile, batch, tt, alpha):
    rows = batch * tt
    folds = tt // TIME_FOLD
    pitch = tt + ROW_PAD

    xb = x_ref[...].reshape(rows, D_MODEL).astype(_BF16)

    def proj(off, width):
        return _dot(xb, w_in_ref[:, off:off + width])

    def fold_inputs(c):
        return utb_ref[:, c * FOLD_COLS:(c + 1) * FOLD_COLS]

    def expand(c):
        cols = []
        for q in range(PAIRS_PER_LANE_TILE):
            j = c * PAIRS_PER_LANE_TILE + q
            cols.append(_dot_t(fold_inputs(c), wbt_ref[j]))
        return jnp.concatenate(cols, axis=1)

    def scan(c, bu):
        l0 = c * BLOCK_COLS
        z = zstate_ref[:, l0:l0 + BLOCK_COLS]
        prev = []
        for k in range(folds):
            prev.append(z.astype(_BF16))
            buk = bu[k * batch:(k + 1) * batch, :]
            out = []
            for q in range(PAIRS_PER_LANE_TILE):
                j = c * PAIRS_PER_LANE_TILE + q
                a_r = ar_ref[j:j + 1, :]
                a_i = ai_ref[j:j + 1, :]
                o = q * PAIR_COLS
                z_r = z[:, o:o + LANES]
                z_i = z[:, o + LANES:o + PAIR_COLS]
                out.append(a_r * z_r - a_i * z_i + buk[:, o:o + LANES])
                out.append(a_r * z_i + a_i * z_r + buk[:, o + LANES:o + PAIR_COLS])
            z = jnp.concatenate(out, axis=1)
        zstate_ref[:, l0:l0 + BLOCK_COLS] = z
        return jnp.concatenate(prev, axis=0)

    def contract(c, s_prev):
        return _dot_t(s_prev, wct_ref[c]) + _dot_t(fold_inputs(c), wdt_ref[c])

    u = proj(_OFF_SSM_IN, SSM_WIDTH)
    pool_in = proj(_OFF_POOL_IN, POOL_WIDTH)

    for c in range(SSM_LANE_TILES):
        for b in range(batch):
            ut_ref[c, b * pitch:b * pitch + tt, :] = (
                u[b * tt:(b + 1) * tt, c * LANES:(c + 1) * LANES])
    for t in range(tt):
        k, e = divmod(t, TIME_FOLD)
        for c in range(SSM_LANE_TILES):
            v = ut_ref[c, pl.ds(t, batch, stride=pitch), :]
            l0 = c * FOLD_COLS + e * LANES
            utb_ref[k * batch:(k + 1) * batch, l0:l0 + LANES] = v.astype(_BF16)

    halo_ref[:, POOL_HALO:POOL_HALO + tt, :] = pool_in.reshape(batch, tt, POOL_WIDTH)
    bu0 = expand(0)
    bu1 = expand(1)

    t_glob = tile * tt + lax.broadcasted_iota(jnp.int32, (1, tt, POOL_GROUP), 1)
    pooled = []
    for gi, win in enumerate(POOL_WINDOWS):
        c0 = gi * POOL_GROUP
        s = halo_ref[:, :, c0:c0 + POOL_GROUP]
        cur = s[:, POOL_HALO:, :]
        k = 1
        while k < win:
            s = s + pltpu.roll(s, k, axis=1)
            k *= 2
        inv_cnt = 1.0 / jnp.minimum(t_glob + 1, win).astype(_F32)
        d = s[:, POOL_HALO:, :] * inv_cnt - cur
        pooled.append(d.reshape(rows, POOL_GROUP) * pool_scale_ref[:, c0:c0 + POOL_GROUP])
    halo_ref[:, 0:POOL_HALO, :] = halo_ref[:, tt:tt + POOL_HALO, :]

    h_pool = proj(_OFF_POOL_GATE, POOL_WIDTH)
    sp0 = scan(0, bu0)
    bu2 = expand(2)
    bu3 = expand(3)
    t_gpool = jnp.tanh(proj(_OFF_G_POOL, D_MODEL))
    sp1 = scan(1, bu1)
    y_fold = [contract(0, sp0)]
    y_pool = (jnp.concatenate(pooled, axis=1) * _half_silu(h_pool)).astype(_BF16)
    y_pool = y_pool + jnp.concatenate([order_zero] * (POOL_WIDTH // LANES), axis=1)
    half_bp = _dot(y_pool, wbp_ref[...])
    sp2 = scan(2, bu2)
    h_ssm = proj(_OFF_SSM_GATE, SSM_WIDTH)
    y_fold.append(contract(1, sp1))
    merged_pool = half_bp + t_gpool * half_bp
    sp3 = scan(3, bu3)
    t_gssm = jnp.tanh(proj(_OFF_G_SSM, D_MODEL))
    y_fold.append(contract(2, sp2))
    y_fold.append(contract(3, sp3))

    for c in range(SSM_LANE_TILES):
        for t in range(tt):
            k, e = divmod(t, TIME_FOLD)
            yt_ref[c, pl.ds(t, batch, stride=pitch), :] = (
                y_fold[c][k * batch:(k + 1) * batch, e * LANES:(e + 1) * LANES])
    y_cols = []
    for c in range(SSM_LANE_TILES):
        y_cols.append(jnp.concatenate(
            [yt_ref[c, b * pitch:b * pitch + tt, :] for b in range(batch)], axis=0))
    y = jnp.concatenate(y_cols, axis=1) + dskip_ref[...] * u
    half_y = 0.5 * y
    g = half_y + half_y * jnp.tanh(y * (_GELU_C + (_GELU_C * 0.044715) * (y * y)))
    half_gate = D_MODEL // 2
    t_ple_lo = jnp.tanh(proj(_OFF_PLE_GATE, half_gate))
    half_q = _dot(g.astype(_BF16), glu_w_ref[...]) + glu_b_ref[...]
    t_ple_hi = jnp.tanh(proj(_OFF_PLE_GATE + half_gate, half_gate))
    t_ple = jnp.concatenate([t_ple_lo, t_ple_hi], axis=1)
    half_g = 0.5 * g
    y_ssm = ((half_g + half_g * jnp.tanh(half_q)) * _half_silu(h_ssm)).astype(_BF16)
    half_bs = _dot(y_ssm, wbs_ref[...])
    pb = p_ref[...].reshape(rows, PLE_DIM).astype(_BF16)
    half_ple = _dot(pb, w_ple_ref[...])
    merged_ref[...] = (merged_pool + half_bs + t_gssm * half_bs).astype(_BF16)
    ple = half_ple + t_ple * half_ple
    base_ref[...] = alpha * x_ref[...].reshape(rows, D_MODEL) + ple


def _cmul(ar, ai, br, bi):
    return ar * br - ai * bi, ar * bi + ai * br


def _spread_lanes(x2d, period):
    onehot = (jnp.arange(LANES)[None, :] % period == jnp.arange(period)[:, None]).astype(_F32)
    return jnp.dot(x2d, onehot, precision=_EXACT)


def _ssm_operands(a_re, a_im, log_dt, b_re, b_im, c_re, c_im):
    f32 = _F32
    dt = jnp.exp(log_dt.astype(f32))[:, None]
    lr = a_re.astype(f32)
    li = a_im.astype(f32)
    mag = jnp.exp(lr * dt)
    abar_r = mag * jnp.cos(li * dt)
    abar_i = mag * jnp.sin(li * dt)
    den = lr * lr + li * li
    zr, zi = _cmul(abar_r - 1.0, abar_i, lr, -li)
    zr = zr / den
    zi = zi / den
    bbar_r, bbar_i = _cmul(zr[..., None], zi[..., None], b_re.astype(f32), b_im.astype(f32))

    a2_r, a2_i = _cmul(abar_r, abar_i, abar_r, abar_i)
    ab_r, ab_i = _cmul(abar_r[..., None], abar_i[..., None], bbar_r, bbar_i)
    cr = c_re.astype(f32)
    ci = c_im.astype(f32)
    ca_r, ca_i = _cmul(cr, ci, abar_r[:, None, :], abar_i[:, None, :])
    ca2_r, ca2_i = _cmul(cr, ci, a2_r[:, None, :], a2_i[:, None, :])

    def re_prod(xr, xi, yr, yi):
        return (jnp.einsum("ghp,gpk->ghk", xr, yr, precision=_EXACT)
                - jnp.einsum("ghp,gpk->ghk", xi, yi, precision=_EXACT))

    cb = re_prod(cr, ci, bbar_r, bbar_i)
    cab = re_prod(ca_r, ca_i, bbar_r, bbar_i)

    ar2 = a2_r.reshape(N_PAIRS, LANES)
    ai2 = a2_i.reshape(N_PAIRS, LANES)

    n_state_rows = N_PAIRS * PAIR_COLS
    row = jnp.arange(n_state_rows)[:, None]
    lane_group = jnp.arange(LANES)[None, :] // SSM_GROUP_CH
    src_group = ((row // PAIR_COLS) % PAIRS_PER_LANE_TILE) * GROUPS_PER_PAIR + (row // SSM_STATE) % 2
    keep_b = lane_group == src_group

    def state_rows(xr, xi):
        shape = (N_PAIRS, GROUPS_PER_PAIR * SSM_STATE, SSM_GROUP_CH)
        x = jnp.stack([xr.reshape(shape), xi.reshape(shape)], axis=1).reshape(
            n_state_rows, SSM_GROUP_CH)
        return jnp.where(keep_b, _spread_lanes(x, SSM_GROUP_CH), 0.0)

    wbt = jnp.concatenate([state_rows(ab_r, ab_i), state_rows(bbar_r, bbar_i)], axis=1)
    wbt = wbt.reshape(N_PAIRS, PAIR_COLS, FOLD_COLS)

    ch = jnp.arange(SSM_WIDTH)[:, None]
    kcol = jnp.arange(BLOCK_COLS)[None, :]
    dst_group = (ch // SSM_GROUP_CH) % GROUPS_PER_LANE_TILE
    col_group = (kcol // PAIR_COLS) * GROUPS_PER_PAIR + (kcol // SSM_STATE) % 2
    keep_c = dst_group == col_group

    def out_rows(xr, xi):
        x2r = xr.reshape(SSM_WIDTH, SSM_STATE)
        x2i = -xi.reshape(SSM_WIDTH, SSM_STATE)
        pair = jnp.concatenate([x2r, x2r, x2i, x2i], axis=1)
        full = jnp.concatenate([pair] * PAIRS_PER_LANE_TILE, axis=1)
        return jnp.where(keep_c, full, 0.0).reshape(SSM_LANE_TILES, LANES, BLOCK_COLS)

    wct = jnp.concatenate([out_rows(ca_r, ca_i), out_rows(ca2_r, ca2_i)], axis=1)

    keep_d = lane_group == dst_group

    def direct(m):
        x = _spread_lanes(m.reshape(SSM_WIDTH, SSM_GROUP_CH), SSM_GROUP_CH)
        return jnp.where(keep_d, x, 0.0).reshape(SSM_LANE_TILES, LANES, LANES)

    d_cb = direct(cb)
    wdt = jnp.concatenate([
        jnp.concatenate([d_cb, jnp.zeros_like(d_cb)], axis=2),
        jnp.concatenate([direct(cab), d_cb], axis=2)], axis=1)
    return ar2, ai2, wbt.astype(_BF16), wct.astype(_BF16), wdt.astype(_BF16)


def _const_spec(shape):
    zeros = (0,) * len(shape)
    return pl.BlockSpec(shape, lambda i, _z=zeros: _z, pipeline_mode=pl.Buffered(1))


def _layer(x, p, w_in, pool_w, pool_scale, ssm, d_skip, glu_w, glu_b,
           w_branch_pool, w_branch_ssm, w_out, w_ple, ln_g, ln_b, alpha):
    batch, seq, d_model = x.shape
    assert d_model == D_MODEL and w_in.shape == (D_MODEL, IN_WIDTH)
    tt = TIME_TILE
    assert seq % tt == 0 and tt % SUBLANES == 0 and tt >= POOL_HALO and tt % TIME_FOLD == 0
    assert batch % SUBLANES == 0
    rows = batch * tt
    pitch = tt + ROW_PAD
    ar2, ai2, wbt, wct, wdt = ssm
    row = lambda v: v.astype(_F32).reshape(1, -1)
    staged = (w_in, w_branch_pool, w_branch_ssm, w_out, w_ple)
    assert all(w.dtype == _F32 for w in staged)
    assert w_in.shape[0] % W_IN_STAGE_ROWS == 0
    assert all(w.shape[0] % WIDE_STAGE_ROWS == 0 and w.shape[1] == D_MODEL for w in staged[1:])
    resident = (
        pool_w.astype(_BF16), row(pool_scale), wbt, wct, wdt, ar2, ai2, row(d_skip),
        (0.5 * glu_w).astype(_BF16), row(0.5 * glu_b), row(ln_g), row(ln_b),
    )
    n_tiles = seq // tt
    mixed_tile = lambda s: (0, jnp.minimum(s, n_tiles - 1), 0)
    finished_tile = lambda s: (0, jnp.maximum(s - 1, 0), 0)
    in_specs = [
        pl.BlockSpec((batch, tt, D_MODEL), mixed_tile),
        pl.BlockSpec((batch, tt, PLE_DIM), mixed_tile),
    ] + [pl.BlockSpec(memory_space=pl.ANY) for _ in staged] + [
        _const_spec(op.shape) for op in resident]
    kernel = functools.partial(_layer_kernel, batch=batch, tt=tt, alpha=alpha)
    return pl.pallas_call(
        kernel,
        grid=(n_tiles + 1,),
        in_specs=in_specs,
        out_specs=pl.BlockSpec((batch, tt, D_MODEL), finished_tile),
        out_shape=jax.ShapeDtypeStruct(x.shape, x.dtype),
        scratch_shapes=[pltpu.VMEM(w.shape, _BF16) for w in staged] + [
            pltpu.VMEM((STAGE_DEPTH, W_IN_STAGE_ROWS, IN_WIDTH), _F32),
            pltpu.VMEM((STAGE_DEPTH, WIDE_STAGE_ROWS, D_MODEL), _F32),
            pltpu.SemaphoreType.DMA((STAGE_DEPTH,)),
            pltpu.SemaphoreType.DMA((STAGE_DEPTH,)),
            pltpu.VMEM((batch, POOL_HALO + tt, POOL_WIDTH), _F32),
            pltpu.VMEM((SSM_LANE_TILES, batch * pitch, LANES), _F32),
            pltpu.VMEM((rows // TIME_FOLD, SSM_LANE_TILES * FOLD_COLS), _BF16),
            pltpu.VMEM((batch, STATE_COLS), _F32),
            pltpu.VMEM((SSM_LANE_TILES, batch * pitch, LANES), _F32),
            pltpu.VMEM((rows, D_MODEL), _BF16),
            pltpu.VMEM((rows, D_MODEL), _F32),
        ],
        compiler_params=pltpu.CompilerParams(
            dimension_semantics=("arbitrary",),
            vmem_limit_bytes=VMEM_LIMIT_BYTES,
        ),
        name="hybrid_pool_s5_layer",
    )(x, p, *staged, *resident)


def kernel(x, p, w_in, pool_w, pool_scale, ssm_a_re, ssm_a_im, ssm_log_dt, ssm_b_re, ssm_b_im, ssm_c_re, ssm_c_im, ssm_d, glu_w, glu_b, w_branch_pool, w_branch_ssm, w_out, w_ple, ln_g, ln_b):
    depth = w_in.shape[0]
    alpha = (2.0 * depth) ** 0.25
    for i in range(depth):
        ssm = _ssm_operands(ssm_a_re[i], ssm_a_im[i], ssm_log_dt[i], ssm_b_re[i],
                            ssm_b_im[i], ssm_c_re[i], ssm_c_im[i])
        x = _layer(x, p[i], w_in[i], pool_w[i], pool_scale[i], ssm, ssm_d[i], glu_w[i],
                   glu_b[i], w_branch_pool[i], w_branch_ssm[i], w_out[i], w_ple[i],
                   ln_g[i], ln_b[i], alpha)
    return x
```

```python
import functools
import math

import jax
import jax.numpy as jnp
from jax import lax
from jax.experimental import pallas as pl
from jax.experimental.pallas import tpu as pltpu

D_MODEL = 1024
PLE_DIM = 256
POOL_WIDTH = D_MODEL
POOL_WINDOWS = (2, 4, 8, 16)
POOL_GROUP = POOL_WIDTH // len(POOL_WINDOWS)
POOL_HALO = max(POOL_WINDOWS)
SSM_WIDTH = D_MODEL // 2
SSM_GROUP_CH = 16
SSM_GROUPS = SSM_WIDTH // SSM_GROUP_CH
SSM_STATE = 64
LN_EPS = 1e-5

_OFF_POOL_IN = 0
_OFF_POOL_GATE = POOL_WIDTH
_OFF_SSM_IN = 2 * POOL_WIDTH
_OFF_SSM_GATE = 2 * POOL_WIDTH + SSM_WIDTH
_OFF_G_POOL = 2 * POOL_WIDTH + 2 * SSM_WIDTH
_OFF_G_SSM = _OFF_G_POOL + D_MODEL
_OFF_PLE_GATE = _OFF_G_SSM + D_MODEL
IN_WIDTH = _OFF_PLE_GATE + D_MODEL

LANES = 128
SUBLANES = 8
VMEM_LIMIT_BYTES = 60 * 1024 * 1024

GROUPS_PER_PAIR = LANES // SSM_STATE
N_PAIRS = SSM_GROUPS // GROUPS_PER_PAIR
PAIR_COLS = 2 * LANES
GROUPS_PER_LANE_TILE = LANES // SSM_GROUP_CH
PAIRS_PER_LANE_TILE = GROUPS_PER_LANE_TILE // GROUPS_PER_PAIR
SSM_LANE_TILES = SSM_WIDTH // LANES
BLOCK_COLS = PAIRS_PER_LANE_TILE * PAIR_COLS
STATE_COLS = SSM_LANE_TILES * BLOCK_COLS
TIME_FOLD = 2
FOLD_COLS = TIME_FOLD * LANES

TIME_TILE = 32
ROW_PAD = 8
W_IN_STAGE_ROWS = 64
WIDE_STAGE_ROWS = 128
STAGE_DEPTH = 4

_BF16 = jnp.bfloat16
_F32 = jnp.float32
_GELU_C = math.sqrt(2.0 / math.pi)
_CONTRACT_LAST = (((1,), (1,)), ((), ()))
_EXACT = lax.Precision.HIGHEST


def _dot(a, b):
    return jnp.dot(a, b, preferred_element_type=_F32)


def _dot_t(a, b_t):
    return lax.dot_general(a, b_t, _CONTRACT_LAST, preferred_element_type=_F32)


def _half_silu(h):
    return h + h * jnp.tanh(h)


def _stage_weights(rings):
    plans = []
    for jobs, stage_ref, sem in rings:
        depth, chunk = stage_ref.shape[0], stage_ref.shape[1]
        chunks = [(src, dst, scale, r0)
                  for src, dst, scale in jobs for r0 in range(0, src.shape[0], chunk)]

        def copy(i, chunks=chunks, chunk=chunk, depth=depth, stage_ref=stage_ref, sem=sem):
            src, _, _, r0 = chunks[i]
            return pltpu.make_async_copy(
                src.at[r0:r0 + chunk, :], stage_ref.at[i % depth], sem.at[i % depth])

        plans.append((chunks, chunk, depth, stage_ref, copy))

    for chunks, _, depth, _, copy in plans:
        for i in range(min(depth, len(chunks))):
            copy(i).start()
    order = sorted(((i + 1) / len(plan[0]), r, i)
                   for r, plan in enumerate(plans) for i in range(len(plan[0])))
    for _, r, i in order:
        chunks, chunk, depth, stage_ref, copy = plans[r]
        _, dst, scale, r0 = chunks[i]
        copy(i).wait()
        dst[r0:r0 + chunk, :] = (stage_ref[i % depth] * scale).astype(_BF16)
        if i + depth < len(chunks):
            copy(i + depth).start()


def _layer_kernel(x_ref, p_ref, w_in_hbm, wbp_hbm, wbs_hbm, w_out_hbm, w_ple_hbm,
                  pool_w_ref, pool_scale_ref, wbt_ref, wct_ref, wdt_ref,
                  ar_ref, ai_ref, dskip_ref, glu_w_ref, glu_b_ref, lng_ref, lnb_ref, o_ref,
                  w_in_ref, wbp_ref, wbs_ref, w_out_ref, w_ple_ref,
                  in_stage_ref, wide_stage_ref, in_sem, wide_sem,
                  halo_ref, ut_ref, utb_ref, zstate_ref, yt_ref, merged_ref, base_ref,
                  *, batch, tt, alpha):
    rows = batch * tt
    step = pl.program_id(0)
    last = pl.num_programs(0) - 1

    @pl.when(step == 0)
    def _():
        halo_ref[:, 0:POOL_HALO, :] = jnp.zeros((batch, POOL_HALO, POOL_WIDTH), _F32)
        zstate_ref[...] = jnp.zeros_like(zstate_ref)
        merged_ref[...] = jnp.zeros_like(merged_ref)
        base_ref[...] = jnp.zeros_like(base_ref)
        col = lax.broadcasted_iota(jnp.int32, (1, IN_WIDTH), 1)
        is_value = (col < _OFF_POOL_GATE) | ((col >= _OFF_SSM_IN) & (col < _OFF_SSM_GATE))
        gate_scale = jnp.where(is_value, 1.0, 0.5).astype(_F32)
        _stage_weights([
            ([(w_in_hbm, w_in_ref, gate_scale)], in_stage_ref, in_sem),
            ([(wbp_hbm, wbp_ref, 0.5), (w_out_hbm, w_out_ref, 1.0),
              (wbs_hbm, wbs_ref, 0.5), (w_ple_hbm, w_ple_ref, 0.5)], wide_stage_ref, wide_sem)])
        for gi in range(len(POOL_WINDOWS)):
            c0 = _OFF_POOL_IN + gi * POOL_GROUP
            w_in_ref[:, c0:c0 + POOL_GROUP] = _dot(
                w_in_ref[:, c0:c0 + POOL_GROUP], pool_w_ref[gi]).astype(_BF16)

    def finish_previous():
        h = base_ref[...] + _dot(merged_ref[...], w_out_ref[...])
        mu = jnp.mean(h, axis=-1, keepdims=True)
        hc = h - mu
        var = jnp.mean(hc * hc, axis=-1, keepdims=True)
        out = hc * lax.rsqrt(var + LN_EPS) * lng_ref[...] + lnb_ref[...]
        o_ref[...] = out.reshape(batch, tt, D_MODEL).astype(o_ref.dtype)
        return out

    @pl.when(step < last)
    def _():
        out_prev = finish_previous()
        bits = lax.bitcast_convert_type(out_prev, jnp.int32)
        folded = bits[:, 0:LANES]
        for c in range(1, D_MODEL // LANES):
            folded = folded | bits[:, c * LANES:(c + 1) * LANES]
        sixteen = jnp.int32(16)
        zero = lax.shift_right_logical(lax.shift_right_logical(folded, sixteen), sixteen)
        _tile(x_ref, p_ref, w_in_ref, pool_scale_ref, wbt_ref, wct_ref, wdt_ref,
              ar_ref, ai_ref, dskip_ref, glu_w_ref, glu_b_ref, wbp_ref, wbs_ref,
              w_ple_ref, halo_ref, ut_ref, utb_ref, zstate_ref, yt_ref, merged_ref, base_ref,
              zero.astype(_F32).astype(_BF16),
              tile=step, batch=batch, tt=tt, alpha=alpha)

    @pl.when(step == last)
    def _():
        finish_previous()


def _tile(x_ref, p_ref, w_in_ref, pool_scale_ref, wbt_ref, wct_ref, wdt_ref,
          ar_ref, ai_ref, dskip_ref, glu_w_ref, glu_b_ref, wbp_ref, wbs_ref,
          w_ple_ref, halo_ref, ut_ref, utb_ref, zstate_ref, yt_ref, merged_ref, base_ref,
          order_zero, *, tile, batch, tt, alpha):
    rows = batch * tt
    folds = tt // TIME_FOLD
    pitch = tt + ROW_PAD

    xb = x_ref[...].reshape(rows, D_MODEL).astype(_BF16)

    def proj(off, width):
        return _dot(xb, w_in_ref[:, off:off + width])

    def fold_inputs(c):
        return utb_ref[:, c * FOLD_COLS:(c + 1) * FOLD_COLS]

    def expand(c):
        cols = []
        for q in range(PAIRS_PER_LANE_TILE):
            j = c * PAIRS_PER_LANE_TILE + q
            cols.append(_dot_t(fold_inputs(c), wbt_ref[j]))
        return jnp.concatenate(cols, axis=1)

    def scan(c, bu):
        l0 = c * BLOCK_COLS
        z = zstate_ref[:, l0:l0 + BLOCK_COLS]
        prev = []
        for k in range(folds):
            prev.append(z.astype(_BF16))
            buk = bu[k * batch:(k + 1) * batch, :]
            out = []
            for q in range(PAIRS_PER_LANE_TILE):
                j = c * PAIRS_PER_LANE_TILE + q
                a_r = ar_ref[j:j + 1, :]
                a_i = ai_ref[j:j + 1, :]
                o = q * PAIR_COLS
                z_r = z[:, o:o + LANES]
                z_i = z[:, o + LANES:o + PAIR_COLS]
                out.append(a_r * z_r - a_i * z_i + buk[:, o:o + LANES])
                out.append(a_r * z_i + a_i * z_r + buk[:, o + LANES:o + PAIR_COLS])
            z = jnp.concatenate(out, axis=1)
        zstate_ref[:, l0:l0 + BLOCK_COLS] = z
        return jnp.concatenate(prev, axis=0)

    def contract(c, s_prev):
        return _dot_t(s_prev, wct_ref[c]) + _dot_t(fold_inputs(c), wdt_ref[c])

    u = proj(_OFF_SSM_IN, SSM_WIDTH)
    pool_in = proj(_OFF_POOL_IN, POOL_WIDTH)

    for c in range(SSM_LANE_TILES):
        for b in range(batch):
            ut_ref[c, b * pitch:b * pitch + tt, :] = (
                u[b * tt:(b + 1) * tt, c * LANES:(c + 1) * LANES])
    for t in range(tt):
        k, e = divmod(t, TIME_FOLD)
        for c in range(SSM_LANE_TILES):
            v = ut_ref[c, pl.ds(t, batch, stride=pitch), :]
            l0 = c * FOLD_COLS + e * LANES
            utb_ref[k * batch:(k + 1) * batch, l0:l0 + LANES] = v.astype(_BF16)

    halo_ref[:, POOL_HALO:POOL_HALO + tt, :] = pool_in.reshape(batch, tt, POOL_WIDTH)
    bu0 = expand(0)
    bu1 = expand(1)

    h_pool = proj(_OFF_POOL_GATE, POOL_WIDTH)
    sp0 = scan(0, bu0)
    bu2 = expand(2)
    bu3 = expand(3)
    t_gpool = jnp.tanh(proj(_OFF_G_POOL, D_MODEL))
    sp1 = scan(1, bu1)
    y_fold = [contract(0, sp0)]
    sp2 = scan(2, bu2)
    h_ssm = proj(_OFF_SSM_GATE, SSM_WIDTH)
    y_fold.append(contract(1, sp1))

    t_glob = tile * tt + lax.broadcasted_iota(jnp.int32, (1, tt, POOL_GROUP), 1)
    pooled = []
    for gi, win in enumerate(POOL_WINDOWS):
        c0 = gi * POOL_GROUP
        s = halo_ref[:, :, c0:c0 + POOL_GROUP]
        cur = s[:, POOL_HALO:, :]
        k = 1
        while k < win:
            s = s + pltpu.roll(s, k, axis=1)
            k *= 2
        inv_cnt = 1.0 / jnp.minimum(t_glob + 1, win).astype(_F32)
        d = s[:, POOL_HALO:, :] * inv_cnt - cur
        pooled.append(d.reshape(rows, POOL_GROUP) * pool_scale_ref[:, c0:c0 + POOL_GROUP])
    halo_ref[:, 0:POOL_HALO, :] = halo_ref[:, tt:tt + POOL_HALO, :]
    y_pool =(jnp.concatenate(pooled, axis=1) * _half_silu(h_pool)).astype(_BF16)
    y_pool = y_pool + jnp.concatenate([order_zero] * (POOL_WIDTH // LANES), axis=1)
    half_bp = _dot(y_pool, wbp_ref[...])
    merged_pool = half_bp + t_gpool * half_bp
    sp3 = scan(3, bu3)
    t_gssm = jnp.tanh(proj(_OFF_G_SSM, D_MODEL))
    y_fold.append(contract(2, sp2))
    y_fold.append(contract(3, sp3))

    for c in range(SSM_LANE_TILES):
        for t in range(tt):
            k, e = divmod(t, TIME_FOLD)
            yt_ref[c, pl.ds(t, batch, stride=pitch), :] = (
                y_fold[c][k * batch:(k + 1) * batch, e * LANES:(e + 1) * LANES])
    y_cols = []
    for c in range(SSM_LANE_TILES):
        y_cols.append(jnp.concatenate(
            [yt_ref[c, b * pitch:b * pitch + tt, :] for b in range(batch)], axis=0))
    y = jnp.concatenate(y_cols, axis=1) + dskip_ref[...] * u
    half_y = 0.5 * y
    g = half_y + half_y * jnp.tanh(y * (_GELU_C + (_GELU_C * 0.044715) * (y * y)))
    half_gate = D_MODEL // 2
    t_ple_lo = jnp.tanh(proj(_OFF_PLE_GATE, half_gate))
    half_q = _dot(g.astype(_BF16), glu_w_ref[...]) + glu_b_ref[...]
    t_ple_hi = jnp.tanh(proj(_OFF_PLE_GATE + half_gate, half_gate))
    t_ple = jnp.concatenate([t_ple_lo, t_ple_hi], axis=1)
    half_g = 0.5 * g
    y_ssm = ((half_g + half_g * jnp.tanh(half_q)) * _half_silu(h_ssm)).astype(_BF16)
    half_bs = _dot(y_ssm, wbs_ref[...])
    pb = p_ref[...].reshape(rows, PLE_DIM).astype(_BF16)
    half_ple = _dot(pb, w_ple_ref[...])
    merged_ref[...] = (merged_pool + half_bs + t_gssm * half_bs).astype(_BF16)
    ple = half_ple + t_ple * half_ple
    base_ref[...] = alpha * x_ref[...].reshape(rows, D_MODEL) + ple


def _cmul(ar, ai, br, bi):
    return ar * br - ai * bi, ar * bi + ai * br


def _spread_lanes(x2d, period):
    onehot = (jnp.arange(LANES)[None, :] % period == jnp.arange(period)[:, None]).astype(_F32)
    return jnp.dot(x2d, onehot, precision=_EXACT)


def _ssm_operands(a_re, a_im, log_dt, b_re, b_im, c_re, c_im):
    f32 = _F32
    dt = jnp.exp(log_dt.astype(f32))[:, None]
    lr = a_re.astype(f32)
    li = a_im.astype(f32)
    mag = jnp.exp(lr * dt)
    abar_r = mag * jnp.cos(li * dt)
    abar_i = mag * jnp.sin(li * dt)
    den = lr * lr + li * li
    zr, zi = _cmul(abar_r - 1.0, abar_i, lr, -li)
    zr = zr / den
    zi = zi / den
    bbar_r, bbar_i = _cmul(zr[..., None], zi[..., None], b_re.astype(f32), b_im.astype(f32))

    a2_r, a2_i = _cmul(abar_r, abar_i, abar_r, abar_i)
    ab_r, ab_i = _cmul(abar_r[..., None], abar_i[..., None], bbar_r, bbar_i)
    cr = c_re.astype(f32)
    ci = c_im.astype(f32)
    ca_r, ca_i = _cmul(cr, ci, abar_r[:, None, :], abar_i[:, None, :])
    ca2_r, ca2_i = _cmul(cr, ci, a2_r[:, None, :], a2_i[:, None, :])

    def re_prod(xr, xi, yr, yi):
        return (jnp.einsum("ghp,gpk->ghk", xr, yr, precision=_EXACT)
                - jnp.einsum("ghp,gpk->ghk", xi, yi, precision=_EXACT))

    cb = re_prod(cr, ci, bbar_r, bbar_i)
    cab = re_prod(ca_r, ca_i, bbar_r, bbar_i)

    ar2 = a2_r.reshape(N_PAIRS, LANES)
    ai2 = a2_i.reshape(N_PAIRS, LANES)

    n_state_rows = N_PAIRS * PAIR_COLS
    row = jnp.arange(n_state_rows)[:, None]
    lane_group = jnp.arange(LANES)[None, :] // SSM_GROUP_CH
    src_group = ((row // PAIR_COLS) % PAIRS_PER_LANE_TILE) * GROUPS_PER_PAIR + (row // SSM_STATE) % 2
    keep_b = lane_group == src_group

    def state_rows(xr, xi):
        shape = (N_PAIRS, GROUPS_PER_PAIR * SSM_STATE, SSM_GROUP_CH)
        x = jnp.stack([xr.reshape(shape), xi.reshape(shape)], axis=1).reshape(
            n_state_rows, SSM_GROUP_CH)
        return jnp.where(keep_b, _spread_lanes(x, SSM_GROUP_CH), 0.0)

    wbt = jnp.concatenate([state_rows(ab_r, ab_i), state_rows(bbar_r, bbar_i)], axis=1)
    wbt = wbt.reshape(N_PAIRS, PAIR_COLS, FOLD_COLS)

    ch = jnp.arange(SSM_WIDTH)[:, None]
    kcol = jnp.arange(BLOCK_COLS)[None, :]
    dst_group = (ch // SSM_GROUP_CH) % GROUPS_PER_LANE_TILE
    col_group = (kcol // PAIR_COLS) * GROUPS_PER_PAIR + (kcol // SSM_STATE) % 2
    keep_c = dst_group == col_group

    def out_rows(xr, xi):
        x2r = xr.reshape(SSM_WIDTH, SSM_STATE)
        x2i = -xi.reshape(SSM_WIDTH, SSM_STATE)
        pair = jnp.concatenate([x2r, x2r, x2i, x2i], axis=1)
        full = jnp.concatenate([pair] * PAIRS_PER_LANE_TILE, axis=1)
        return jnp.where(keep_c, full, 0.0).reshape(SSM_LANE_TILES, LANES, BLOCK_COLS)

    wct = jnp.concatenate([out_rows(ca_r, ca_i), out_rows(ca2_r, ca2_i)], axis=1)

    keep_d = lane_group == dst_group

    def direct(m):
        x = _spread_lanes(m.reshape(SSM_WIDTH, SSM_GROUP_CH), SSM_GROUP_CH)
        return jnp.where(keep_d, x, 0.0).reshape(SSM_LANE_TILES, LANES, LANES)

    d_cb = direct(cb)
    wdt = jnp.concatenate([
        jnp.concatenate([d_cb, jnp.zeros_like(d_cb)], axis=2),
        jnp.concatenate([direct(cab), d_cb], axis=2)], axis=1)
    return ar2, ai2, wbt.astype(_BF16), wct.astype(_BF16), wdt.astype(_BF16)


def _const_spec(shape):
    zeros = (0,) * len(shape)
    return pl.BlockSpec(shape, lambda i, _z=zeros: _z, pipeline_mode=pl.Buffered(1))


def _layer(x, p, w_in, pool_w, pool_scale, ssm, d_skip, glu_w, glu_b,
           w_branch_pool, w_branch_ssm, w_out, w_ple, ln_g, ln_b, alpha):
    batch, seq, d_model = x.shape
    assert d_model == D_MODEL and w_in.shape == (D_MODEL, IN_WIDTH)
    tt = TIME_TILE
    assert seq % tt == 0 and tt % SUBLANES == 0 and tt >= POOL_HALO and tt % TIME_FOLD == 0
    assert batch % SUBLANES == 0
    rows = batch * tt
    pitch = tt + ROW_PAD
    ar2, ai2, wbt, wct, wdt = ssm
    row = lambda v: v.astype(_F32).reshape(1, -1)
    staged = (w_in, w_branch_pool, w_branch_ssm, w_out, w_ple)
    assert all(w.dtype == _F32 for w in staged)
    assert w_in.shape[0] % W_IN_STAGE_ROWS == 0
    assert all(w.shape[0] % WIDE_STAGE_ROWS == 0 and w.shape[1] == D_MODEL for w in staged[1:])
    resident = (
        pool_w.astype(_BF16), row(pool_scale), wbt, wct, wdt, ar2, ai2, row(d_skip),
        (0.5 * glu_w).astype(_BF16), row(0.5 * glu_b), row(ln_g), row(ln_b),
    )
    n_tiles = seq // tt
    mixed_tile = lambda s: (0, jnp.minimum(s, n_tiles - 1), 0)
    finished_tile = lambda s: (0, jnp.maximum(s - 1, 0), 0)
    in_specs = [
        pl.BlockSpec((batch, tt, D_MODEL), mixed_tile),
        pl.BlockSpec((batch, tt, PLE_DIM), mixed_tile),
    ] + [pl.BlockSpec(memory_space=pl.ANY) for _ in staged] + [
        _const_spec(op.shape) for op in resident]
    kernel = functools.partial(_layer_kernel, batch=batch, tt=tt, alpha=alpha)
    return pl.pallas_call(
        kernel,
        grid=(n_tiles + 1,),
        in_specs=in_specs,
        out_specs=pl.BlockSpec((batch, tt, D_MODEL), finished_tile),
        out_shape=jax.ShapeDtypeStruct(x.shape, x.dtype),
        scratch_shapes=[pltpu.VMEM(w.shape, _BF16) for w in staged] + [
            pltpu.VMEM((STAGE_DEPTH, W_IN_STAGE_ROWS, IN_WIDTH), _F32),
            pltpu.VMEM((STAGE_DEPTH, WIDE_STAGE_ROWS, D_MODEL), _F32),
            pltpu.SemaphoreType.DMA((STAGE_DEPTH,)),
            pltpu.SemaphoreType.DMA((STAGE_DEPTH,)),
            pltpu.VMEM((batch, POOL_HALO + tt, POOL_WIDTH), _F32),
            pltpu.VMEM((SSM_LANE_TILES, batch * pitch, LANES), _F32),
            pltpu.VMEM((rows // TIME_FOLD, SSM_LANE_TILES * FOLD_COLS), _BF16),
            pltpu.VMEM((batch, STATE_COLS), _F32),
            pltpu.VMEM((SSM_LANE_TILES, batch * pitch, LANES), _F32),
            pltpu.VMEM((rows, D_MODEL), _BF16),
            pltpu.VMEM((rows, D_MODEL), _F32),
        ],
        compiler_params=pltpu.CompilerParams(
            dimension_semantics=("arbitrary",),
            vmem_limit_bytes=VMEM_LIMIT_BYTES,
        ),
        name="hybrid_pool_s5_layer",
    )(x, p, *staged, *resident)


def kernel(x, p, w_in, pool_w, pool_scale, ssm_a_re, ssm_a_im, ssm_log_dt, ssm_b_re, ssm_b_im, ssm_c_re, ssm_c_im, ssm_d, glu_w, glu_b, w_branch_pool, w_branch_ssm, w_out, w_ple, ln_g, ln_b):
    depth = w_in.shape[0]
    alpha = (2.0 * depth) ** 0.25
    for i in range(depth):
        ssm = _ssm_operands(ssm_a_re[i], ssm_a_im[i], ssm_log_dt[i], ssm_b_re[i],
                            ssm_b_im[i], ssm_c_re[i], ssm_c_im[i])
        x = _layer(x, p[i], w_in[i], pool_w[i], pool_scale[i], ssm, ssm_d[i], glu_w[i],
                   glu_b[i], w_branch_pool[i], w_branch_ssm[i], w_out[i], w_ple[i],
                   ln_g[i], ln_b[i], alpha)
    return x
```

```python
import functools
import math

import jax
import jax.numpy as jnp
from jax import lax
from jax.experimental import pallas as pl
from jax.experimental.pallas import tpu as pltpu

D_MODEL = 1024
PLE_DIM = 256
POOL_WIDTH = D_MODEL
POOL_WINDOWS = (2, 4, 8, 16)
POOL_GROUP = POOL_WIDTH // len(POOL_WINDOWS)
POOL_HALO = max(POOL_WINDOWS)
SSM_WIDTH = D_MODEL // 2
SSM_GROUP_CH = 16
SSM_GROUPS = SSM_WIDTH // SSM_GROUP_CH
SSM_STATE = 64
LN_EPS = 1e-5

_OFF_POOL_IN = 0
_OFF_POOL_GATE = POOL_WIDTH
_OFF_SSM_IN = 2 * POOL_WIDTH
_OFF_SSM_GATE = 2 * POOL_WIDTH + SSM_WIDTH
_OFF_G_POOL = 2 * POOL_WIDTH + 2 * SSM_WIDTH
_OFF_G_SSM = _OFF_G_POOL + D_MODEL
_OFF_PLE_GATE = _OFF_G_SSM + D_MODEL
IN_WIDTH = _OFF_PLE_GATE + D_MODEL

LANES = 128
SUBLANES = 8
VMEM_LIMIT_BYTES = 60 * 1024 * 1024

GROUPS_PER_PAIR = LANES // SSM_STATE
N_PAIRS = SSM_GROUPS // GROUPS_PER_PAIR
PAIR_COLS = 2 * LANES
GROUPS_PER_LANE_TILE = LANES // SSM_GROUP_CH
PAIRS_PER_LANE_TILE = GROUPS_PER_LANE_TILE // GROUPS_PER_PAIR
SSM_LANE_TILES = SSM_WIDTH // LANES
BLOCK_COLS = PAIRS_PER_LANE_TILE * PAIR_COLS
STATE_COLS = SSM_LANE_TILES * BLOCK_COLS
TIME_FOLD = 2
FOLD_COLS = TIME_FOLD * LANES

TIME_TILE = 32
ROW_PAD = 8
W_IN_STAGE_ROWS = 64
WIDE_STAGE_ROWS = 128
STAGE_DEPTH = 4

_BF16 = jnp.bfloat16
_F32 = jnp.float32
_GELU_C = math.sqrt(2.0 / math.pi)
_CONTRACT_LAST = (((1,), (1,)), ((), ()))
_EXACT = lax.Precision.HIGHEST


def _dot(a, b):
    return jnp.dot(a, b, preferred_element_type=_F32)


def _dot_t(a, b_t):
    return lax.dot_general(a, b_t, _CONTRACT_LAST, preferred_element_type=_F32)


def _half_silu(h):
    return h + h * jnp.tanh(h)


def _stage_weights(rings):
    plans = []
    for jobs, stage_ref, sem in rings:
        depth, chunk = stage_ref.shape[0], stage_ref.shape[1]
        chunks = [(src, dst, scale, r0)
                  for src, dst, scale in jobs for r0 in range(0, src.shape[0], chunk)]

        def copy(i, chunks=chunks, chunk=chunk, depth=depth, stage_ref=stage_ref, sem=sem):
            src, _, _, r0 = chunks[i]
            return pltpu.make_async_copy(
                src.at[r0:r0 + chunk, :], stage_ref.at[i % depth], sem.at[i % depth])

        plans.append((chunks, chunk, depth, stage_ref, copy))

    for chunks, _, depth, _, copy in plans:
        for i in range(min(depth, len(chunks))):
            copy(i).start()
    order = sorted(((i + 1) / len(plan[0]), r, i)
                   for r, plan in enumerate(plans) for i in range(len(plan[0])))
    for _, r, i in order:
        chunks, chunk, depth, stage_ref, copy = plans[r]
        _, dst, scale, r0 = chunks[i]
        copy(i).wait()
        dst[r0:r0 + chunk, :] = (stage_ref[i % depth] * scale).astype(_BF16)
        if i + depth < len(chunks):
            copy(i + depth).start()


def _layer_kernel(x_ref, p_ref, w_in_hbm, wbp_hbm, wbs_hbm, w_out_hbm, w_ple_hbm,
                  pool_w_ref, pool_scale_ref, wbt_ref, wct_ref, wdt_ref,
                  ar_ref, ai_ref, glu_w_ref, glu_b_ref, lng_ref, lnb_ref, o_ref,
                  w_in_ref, wbp_ref, wbs_ref, w_out_ref, w_ple_ref,
                  in_stage_ref, wide_stage_ref, in_sem, wide_sem,
                  halo_ref, ut_ref, utb_ref, zstate_ref, yt_ref, merged_ref, base_ref,
                  *, batch, tt, alpha):
    rows = batch * tt
    step = pl.program_id(0)
    last = pl.num_programs(0) - 1

    @pl.when(step == 0)
    def _():
        halo_ref[:, 0:POOL_HALO, :] = jnp.zeros((batch, POOL_HALO, POOL_WIDTH), _F32)
        zstate_ref[...] = jnp.zeros_like(zstate_ref)
        merged_ref[...] = jnp.zeros_like(merged_ref)
        base_ref[...] = jnp.zeros_like(base_ref)
        col = lax.broadcasted_iota(jnp.int32, (1, IN_WIDTH), 1)
        is_value = (col < _OFF_POOL_GATE) | ((col >= _OFF_SSM_IN) & (col < _OFF_SSM_GATE))
        gate_scale = jnp.where(is_value, 1.0, 0.5).astype(_F32)
        _stage_weights([
            ([(w_in_hbm, w_in_ref, gate_scale)], in_stage_ref, in_sem),
            ([(wbp_hbm, wbp_ref, 0.5), (w_out_hbm, w_out_ref, 1.0),
              (wbs_hbm, wbs_ref, 0.5), (w_ple_hbm, w_ple_ref, 0.5)], wide_stage_ref, wide_sem)])
        for gi in range(len(POOL_WINDOWS)):
            c0 = _OFF_POOL_IN + gi * POOL_GROUP
            folded = _dot(w_in_ref[:, c0:c0 + POOL_GROUP], pool_w_ref[gi])
            scale = pool_scale_ref[:, gi * POOL_GROUP:(gi + 1) * POOL_GROUP]
            w_in_ref[:, c0:c0 + POOL_GROUP] = (folded * scale).astype(_BF16)

    def finish_previous():
        h = base_ref[...] + _dot(merged_ref[...], w_out_ref[...])
        mu = jnp.mean(h, axis=-1, keepdims=True)
        hc = h - mu
        var = jnp.mean(hc * hc, axis=-1, keepdims=True)
        out = hc * lax.rsqrt(var + LN_EPS) * lng_ref[...] + lnb_ref[...]
        o_ref[...] = out.reshape(batch, tt, D_MODEL).astype(o_ref.dtype)
        return out

    @pl.when(step < last)
    def _():
        out_prev = finish_previous()
        bits = lax.bitcast_convert_type(out_prev, jnp.int32)
        folded = bits[:, 0:LANES]
        for c in range(1, D_MODEL // LANES):
            folded = folded | bits[:, c * LANES:(c + 1) * LANES]
        sixteen = jnp.int32(16)
        zero = lax.shift_right_logical(lax.shift_right_logical(folded, sixteen), sixteen)
        _tile(x_ref, p_ref, w_in_ref, wbt_ref, wct_ref, wdt_ref,
              ar_ref, ai_ref, glu_w_ref, glu_b_ref, wbp_ref, wbs_ref,
              w_ple_ref, halo_ref, ut_ref, utb_ref, zstate_ref, yt_ref, merged_ref, base_ref,
              zero.astype(_F32).astype(_BF16),
              tile=step, batch=batch, tt=tt, alpha=alpha)

    @pl.when(step == last)
    def _():
        finish_previous()


def _tile(x_ref, p_ref, w_in_ref, wbt_ref, wct_ref, wdt_ref,
          ar_ref, ai_ref, glu_w_ref, glu_b_ref, wbp_ref, wbs_ref,
          w_ple_ref, halo_ref, ut_ref, utb_ref, zstate_ref, yt_ref, merged_ref, base_ref,
          order_zero, *, tile, batch, tt, alpha):
    rows = batch * tt
    folds = tt // TIME_FOLD
    pitch = tt + ROW_PAD

    xb = x_ref[...].reshape(rows, D_MODEL).astype(_BF16)

    def proj(off, width):
        return _dot(xb, w_in_ref[:, off:off + width])

    def fold_inputs(c):
        return utb_ref[:, c * FOLD_COLS:(c + 1) * FOLD_COLS]

    def expand(c):
        cols = []
        for q in range(PAIRS_PER_LANE_TILE):
            j = c * PAIRS_PER_LANE_TILE + q
            cols.append(_dot_t(fold_inputs(c), wbt_ref[j]))
        return jnp.concatenate(cols, axis=1)

    def scan(c, bu):
        l0 = c * BLOCK_COLS
        z = zstate_ref[:, l0:l0 + BLOCK_COLS]
        prev = []
        for k in range(folds):
            prev.append(z.astype(_BF16))
            buk = bu[k * batch:(k + 1) * batch, :]
            out = []
            for q in range(PAIRS_PER_LANE_TILE):
                j = c * PAIRS_PER_LANE_TILE + q
                a_r = ar_ref[j:j + 1, :]
                a_i = ai_ref[j:j + 1, :]
                o = q * PAIR_COLS
                z_r = z[:, o:o + LANES]
                z_i = z[:, o + LANES:o + PAIR_COLS]
                out.append(a_r * z_r - a_i * z_i + buk[:, o:o + LANES])
                out.append(a_r * z_i + a_i * z_r + buk[:, o + LANES:o + PAIR_COLS])
            z = jnp.concatenate(out, axis=1)
        zstate_ref[:, l0:l0 + BLOCK_COLS] = z
        return jnp.concatenate(prev, axis=0)

    def contract(c, s_prev):
        return _dot_t(s_prev, wct_ref[c]) + _dot_t(fold_inputs(c), wdt_ref[c])

    u = proj(_OFF_SSM_IN, SSM_WIDTH)
    pool_in = proj(_OFF_POOL_IN, POOL_WIDTH)

    for c in range(SSM_LANE_TILES):
        for b in range(batch):
            ut_ref[c, b * pitch:b * pitch + tt, :] = (
                u[b * tt:(b + 1) * tt, c * LANES:(c + 1) * LANES])
    for t in range(tt):
        k, e = divmod(t, TIME_FOLD)
        for c in range(SSM_LANE_TILES):
            v = ut_ref[c, pl.ds(t, batch, stride=pitch), :]
            l0 = c * FOLD_COLS + e * LANES
            utb_ref[k * batch:(k + 1) * batch, l0:l0 + LANES] = v.astype(_BF16)

    halo_ref[:, POOL_HALO:POOL_HALO + tt, :] = pool_in.reshape(batch, tt, POOL_WIDTH)
    bu0 = expand(0)
    bu1 = expand(1)

    t_glob = tile * tt + lax.broadcasted_iota(jnp.int32, (1, tt, POOL_GROUP), 1)
    pooled = []
    for gi, win in enumerate(POOL_WINDOWS):
        c0 = gi * POOL_GROUP
        s = halo_ref[:, :, c0:c0 + POOL_GROUP]
        cur = s[:, POOL_HALO:, :]
        k = 1
        while k < win:
            s = s + pltpu.roll(s, k, axis=1)
            k *= 2
        inv_cnt = 1.0 / jnp.minimum(t_glob + 1, win).astype(_F32)
        d = s[:, POOL_HALO:, :] * inv_cnt - cur
        pooled.append(d.reshape(rows, POOL_GROUP))
    halo_ref[:, 0:POOL_HALO, :] = halo_ref[:, tt:tt + POOL_HALO, :]

    h_pool = proj(_OFF_POOL_GATE, POOL_WIDTH)
    sp0 = scan(0, bu0)
    bu2 = expand(2)
    bu3 = expand(3)
    t_gpool = jnp.tanh(proj(_OFF_G_POOL, D_MODEL))
    sp1 = scan(1, bu1)
    y_fold = [contract(0, sp0)]
    y_pool = (jnp.concatenate(pooled, axis=1) * _half_silu(h_pool)).astype(_BF16)
    y_pool = y_pool + jnp.concatenate([order_zero] * (POOL_WIDTH // LANES), axis=1)
    half_bp = _dot(y_pool, wbp_ref[...])
    sp2 = scan(2, bu2)
    h_ssm = proj(_OFF_SSM_GATE, SSM_WIDTH)
    y_fold.append(contract(1, sp1))
    merged_pool = half_bp + t_gpool * half_bp
    sp3 = scan(3, bu3)
    t_gssm = jnp.tanh(proj(_OFF_G_SSM, D_MODEL))
    y_fold.append(contract(2, sp2))
    y_fold.append(contract(3, sp3))

    for c in range(SSM_LANE_TILES):
        for t in range(tt):
            k, e = divmod(t, TIME_FOLD)
            yt_ref[c, pl.ds(t, batch, stride=pitch), :] = (
                y_fold[c][k * batch:(k + 1) * batch, e * LANES:(e + 1) * LANES])
    y_cols = []
    for c in range(SSM_LANE_TILES):
        y_cols.append(jnp.concatenate(
            [yt_ref[c, b * pitch:b * pitch + tt, :] for b in range(batch)], axis=0))
    y = jnp.concatenate(y_cols, axis=1)
    half_y = 0.5 * y
    g = half_y + half_y * jnp.tanh(y * (_GELU_C + (_GELU_C * 0.044715) * (y * y)))
    half_gate = D_MODEL // 2
    t_ple_lo = jnp.tanh(proj(_OFF_PLE_GATE, half_gate))
    half_q = _dot(g.astype(_BF16), glu_w_ref[...]) + glu_b_ref[...]
    t_ple_hi = jnp.tanh(proj(_OFF_PLE_GATE + half_gate, half_gate))
    t_ple = jnp.concatenate([t_ple_lo, t_ple_hi], axis=1)
    half_g = 0.5 * g
    y_ssm = ((half_g + half_g * jnp.tanh(half_q)) * _half_silu(h_ssm)).astype(_BF16)
    half_bs = _dot(y_ssm, wbs_ref[...])
    pb = p_ref[...].reshape(rows, PLE_DIM).astype(_BF16)
    half_ple = _dot(pb, w_ple_ref[...])
    merged_ref[...] = (merged_pool + half_bs + t_gssm * half_bs).astype(_BF16)
    ple = half_ple + t_ple * half_ple
    base_ref[...] = alpha * x_ref[...].reshape(rows, D_MODEL) + ple


def _cmul(ar, ai, br, bi):
    return ar * br - ai * bi, ar * bi + ai * br


def _spread_lanes(x2d, period):
    onehot = (jnp.arange(LANES)[None, :] % period == jnp.arange(period)[:, None]).astype(_F32)
    return jnp.dot(x2d, onehot, precision=_EXACT)


def _ssm_operands(a_re, a_im, log_dt, b_re, b_im, c_re, c_im, d_skip):
    f32 = _F32
    dt = jnp.exp(log_dt.astype(f32))[:, None]
    lr = a_re.astype(f32)
    li = a_im.astype(f32)
    mag = jnp.exp(lr * dt)
    abar_r = mag * jnp.cos(li * dt)
    abar_i = mag * jnp.sin(li * dt)
    den = lr * lr + li * li
    zr, zi = _cmul(abar_r - 1.0, abar_i, lr, -li)
    zr = zr / den
    zi = zi / den
    bbar_r, bbar_i = _cmul(zr[..., None], zi[..., None], b_re.astype(f32), b_im.astype(f32))

    a2_r, a2_i = _cmul(abar_r, abar_i, abar_r, abar_i)
    ab_r, ab_i = _cmul(abar_r[..., None], abar_i[..., None], bbar_r, bbar_i)
    cr = c_re.astype(f32)
    ci = c_im.astype(f32)
    ca_r, ca_i = _cmul(cr, ci, abar_r[:, None, :], abar_i[:, None, :])
    ca2_r, ca2_i = _cmul(cr, ci, a2_r[:, None, :], a2_i[:, None, :])

    def re_prod(xr, xi, yr, yi):
        return (jnp.einsum("ghp,gpk->ghk", xr, yr, precision=_EXACT)
                - jnp.einsum("ghp,gpk->ghk", xi, yi, precision=_EXACT))

    cb = re_prod(cr, ci, bbar_r, bbar_i)
    cb_skip = cb + jnp.eye(SSM_GROUP_CH, dtype=f32) * d_skip.astype(f32).reshape(
        SSM_GROUPS, SSM_GROUP_CH, 1)
    cab = re_prod(ca_r, ca_i, bbar_r, bbar_i)

    ar2 = a2_r.reshape(N_PAIRS, LANES)
    ai2 = a2_i.reshape(N_PAIRS, LANES)

    n_state_rows = N_PAIRS * PAIR_COLS
    row = jnp.arange(n_state_rows)[:, None]
    lane_group = jnp.arange(LANES)[None, :] // SSM_GROUP_CH
    src_group = ((row // PAIR_COLS) % PAIRS_PER_LANE_TILE) * GROUPS_PER_PAIR + (row // SSM_STATE) % 2
    keep_b = lane_group == src_group

    def state_rows(xr, xi):
        shape = (N_PAIRS, GROUPS_PER_PAIR * SSM_STATE, SSM_GROUP_CH)
        x = jnp.stack([xr.reshape(shape), xi.reshape(shape)], axis=1).reshape(
            n_state_rows, SSM_GROUP_CH)
        return jnp.where(keep_b, _spread_lanes(x, SSM_GROUP_CH), 0.0)

    wbt = jnp.concatenate([state_rows(ab_r, ab_i), state_rows(bbar_r, bbar_i)], axis=1)
    wbt = wbt.reshape(N_PAIRS, PAIR_COLS, FOLD_COLS)

    ch = jnp.arange(SSM_WIDTH)[:, None]
    kcol = jnp.arange(BLOCK_COLS)[None, :]
    dst_group = (ch // SSM_GROUP_CH) % GROUPS_PER_LANE_TILE
    col_group = (kcol // PAIR_COLS) * GROUPS_PER_PAIR + (kcol // SSM_STATE) % 2
    keep_c = dst_group == col_group

    def out_rows(xr, xi):
        x2r = xr.reshape(SSM_WIDTH, SSM_STATE)
        x2i = -xi.reshape(SSM_WIDTH, SSM_STATE)
        pair = jnp.concatenate([x2r, x2r, x2i, x2i], axis=1)
        full = jnp.concatenate([pair] * PAIRS_PER_LANE_TILE, axis=1)
        return jnp.where(keep_c, full, 0.0).reshape(SSM_LANE_TILES, LANES, BLOCK_COLS)

    wct = jnp.concatenate([out_rows(ca_r, ca_i), out_rows(ca2_r, ca2_i)], axis=1)

    keep_d = lane_group == dst_group

    def direct(m):
        x = _spread_lanes(m.reshape(SSM_WIDTH, SSM_GROUP_CH), SSM_GROUP_CH)
        return jnp.where(keep_d, x, 0.0).reshape(SSM_LANE_TILES, LANES, LANES)

    d_cb = direct(cb_skip)
    wdt = jnp.concatenate([
        jnp.concatenate([d_cb, jnp.zeros_like(d_cb)], axis=2),
        jnp.concatenate([direct(cab), d_cb], axis=2)], axis=1)
    return ar2, ai2, wbt.astype(_BF16), wct.astype(_BF16), wdt.astype(_BF16)


def _const_spec(shape):
    zeros = (0,) * len(shape)
    return pl.BlockSpec(shape, lambda i, _z=zeros: _z, pipeline_mode=pl.Buffered(1))


def _layer(x, p, w_in, pool_w, pool_scale, ssm, d_skip, glu_w, glu_b,
           w_branch_pool, w_branch_ssm, w_out, w_ple, ln_g, ln_b, alpha):
    batch, seq, d_model = x.shape
    assert d_model == D_MODEL and w_in.shape == (D_MODEL, IN_WIDTH)
    tt = TIME_TILE
    assert seq % tt == 0 and tt % SUBLANES == 0 and tt >= POOL_HALO and tt % TIME_FOLD == 0
    assert batch % SUBLANES == 0
    rows = batch * tt
    pitch = tt + ROW_PAD
    ar2, ai2, wbt, wct, wdt = ssm
    row = lambda v: v.astype(_F32).reshape(1, -1)
    staged = (w_in, w_branch_pool, w_branch_ssm, w_out, w_ple)
    assert all(w.dtype == _F32 for w in staged)
    assert w_in.shape[0] % W_IN_STAGE_ROWS == 0
    assert all(w.shape[0] % WIDE_STAGE_ROWS == 0 and w.shape[1] == D_MODEL for w in staged[1:])
    resident = (
        pool_w.astype(_BF16), row(pool_scale), wbt, wct, wdt, ar2, ai2,
        (0.5 * glu_w).astype(_BF16), row(0.5 * glu_b), row(ln_g), row(ln_b),
    )
    n_tiles = seq // tt
    mixed_tile = lambda s: (0, jnp.minimum(s, n_tiles - 1), 0)
    finished_tile = lambda s: (0, jnp.maximum(s - 1, 0), 0)
    in_specs = [
        pl.BlockSpec((batch, tt, D_MODEL), mixed_tile),
        pl.BlockSpec((batch, tt, PLE_DIM), mixed_tile),
    ] + [pl.BlockSpec(memory_space=pl.ANY) for _ in staged] + [
        _const_spec(op.shape) for op in resident]
    kernel = functools.partial(_layer_kernel, batch=batch, tt=tt, alpha=alpha)
    return pl.pallas_call(
        kernel,
        grid=(n_tiles + 1,),
        in_specs=in_specs,
        out_specs=pl.BlockSpec((batch, tt, D_MODEL), finished_tile),
        out_shape=jax.ShapeDtypeStruct(x.shape, x.dtype),
        scratch_shapes=[pltpu.VMEM(w.shape, _BF16) for w in staged] + [
            pltpu.VMEM((STAGE_DEPTH, W_IN_STAGE_ROWS, IN_WIDTH), _F32),
            pltpu.VMEM((STAGE_DEPTH, WIDE_STAGE_ROWS, D_MODEL), _F32),
            pltpu.SemaphoreType.DMA((STAGE_DEPTH,)),
            pltpu.SemaphoreType.DMA((STAGE_DEPTH,)),
            pltpu.VMEM((batch, POOL_HALO + tt, POOL_WIDTH), _F32),
            pltpu.VMEM((SSM_LANE_TILES, batch * pitch, LANES), _F32),
            pltpu.VMEM((rows // TIME_FOLD, SSM_LANE_TILES * FOLD_COLS), _BF16),
            pltpu.VMEM((batch, STATE_COLS), _F32),
            pltpu.VMEM((SSM_LANE_TILES, batch * pitch, LANES), _F32),
            pltpu.VMEM((rows, D_MODEL), _BF16),
            pltpu.VMEM((rows, D_MODEL), _F32),
        ],
        compiler_params=pltpu.CompilerParams(
            dimension_semantics=("arbitrary",),
            vmem_limit_bytes=VMEM_LIMIT_BYTES,
        ),
        name="hybrid_pool_s5_layer",
    )(x, p, *staged, *resident)


def kernel(x, p, w_in, pool_w, pool_scale, ssm_a_re, ssm_a_im, ssm_log_dt, ssm_b_re, ssm_b_im, ssm_c_re, ssm_c_im, ssm_d, glu_w, glu_b, w_branch_pool, w_branch_ssm, w_out, w_ple, ln_g, ln_b):
    depth = w_in.shape[0]
    alpha = (2.0 * depth) ** 0.25
    for i in range(depth):
        ssm = _ssm_operands(ssm_a_re[i], ssm_a_im[i], ssm_log_dt[i], ssm_b_re[i],
                            ssm_b_im[i], ssm_c_re[i], ssm_c_im[i], ssm_d[i])
        x = _layer(x, p[i], w_in[i], pool_w[i], pool_scale[i], ssm, ssm_d[i], glu_w[i],
                   glu_b[i], w_branch_pool[i], w_branch_ssm[i], w_out[i], w_ple[i],
                   ln_g[i], ln_b[i], alpha)
    return x
```
